```python
import math
import jax, jax.numpy as jnp
from jax import lax
import numpy as np

D_MODEL = 1024
BATCH = 16
SEQ = 4096
DEPTH = 4

N_MIXERS = 2
PLE_DIM = 256
FFN_HIDDEN = ((8 * D_MODEL // 3 + 255) // 256) * 256
DA_HEADS = 8
DA_HEAD_DIM = D_MODEL // (2 * DA_HEADS)
Q_BLOCK = 128
NA_HEADS = 16
NA_HEAD_DIM = D_MODEL // NA_HEADS
GRID_W = 64
NA_KR = 8
NA_KC = 16
NORM_EPS = 1e-6
SUBLN_EPS = 1e-5

kernel_name = "hybrid_diffattn_natten_encoder"


def rmsnorm(x, g, eps=NORM_EPS):
    xf = x.astype(jnp.float32)
    y = xf * lax.rsqrt(jnp.mean(xf * xf, axis=-1, keepdims=True) + eps)
    return (y * g.astype(jnp.float32)).astype(x.dtype)


def alibi_slopes(n_heads):
    return jnp.array([2.0 ** (-8.0 * (h + 1) / n_heads) for h in range(n_heads)], dtype=jnp.float32)


def diff_attention(h, w_qkv, w_o, lq1, lk1, lq2, lk2, subln, layer_idx):
    B, S, _ = h.shape
    qkv = h @ w_qkv
    q, k, v = jnp.split(qkv, 3, axis=-1)
    q = q.reshape(B, S, DA_HEADS, 2, DA_HEAD_DIM).transpose(3, 0, 2, 1, 4)
    k = k.reshape(B, S, DA_HEADS, 2, DA_HEAD_DIM).transpose(3, 0, 2, 1, 4)
    v = v.reshape(B, S, DA_HEADS, 2 * DA_HEAD_DIM).transpose(0, 2, 1, 3)
    lam_init = 0.8 - 0.6 * math.exp(-0.3 * layer_idx)
    lam = (jnp.exp(jnp.sum(lq1.astype(jnp.float32) * lk1.astype(jnp.float32)))
           - jnp.exp(jnp.sum(lq2.astype(jnp.float32) * lk2.astype(jnp.float32)))
           + lam_init)
    slopes = alibi_slopes(DA_HEADS)
    scale = DA_HEAD_DIM ** -0.5
    nblk = S // Q_BLOCK
    qb = q.reshape(2, B, DA_HEADS, nblk, Q_BLOCK, DA_HEAD_DIM).transpose(3, 0, 1, 2, 4, 5)
    kpos = jnp.arange(S)

    def block(args):
        qblk, bi = args
        qpos = bi * Q_BLOCK + jnp.arange(Q_BLOCK)
        dist = jnp.abs(qpos[:, None] - kpos[None, :]).astype(jnp.float32)
        bias = -slopes[:, None, None] * dist
        s = jnp.einsum('mbhqd,mbhkd->mbhqk', qblk, k).astype(jnp.float32) * scale + bias
        a = jax.nn.softmax(s, axis=-1)
        a = a[0] - lam * a[1]
        return jnp.einsum('bhqk,bhkd->bhqd', a.astype(v.dtype), v)

    o = lax.map(block, (qb, jnp.arange(nblk)))
    o = o.transpose(1, 0, 3, 2, 4).reshape(B, S, DA_HEADS, 2 * DA_HEAD_DIM)
    o = rmsnorm(o, subln, eps=SUBLN_EPS) * (1.0 - lam_init)
    return o.reshape(B, S, D_MODEL) @ w_o


def neighborhood_attention(h, w_qkv, b_qkv, rpb, w_o):
    B, S, _ = h.shape
    rows = S // GRID_W
    kr = min(NA_KR, rows)
    W = GRID_W
    qkv = h @ w_qkv + b_qkv
    qkv = qkv.reshape(B, rows, W, 3, NA_HEADS, NA_HEAD_DIM).transpose(3, 0, 4, 1, 2, 5)
    q = qkv[0] * (NA_HEAD_DIM ** -0.5)
    k = qkv[1]
    v = qkv[2]
    cols = jnp.arange(W)
    cstart = jnp.clip(cols - NA_KC // 2, 0, W - NA_KC)
    col_mask = (cols[None, :] >= cstart[:, None]) & (cols[None, :] < cstart[:, None] + NA_KC)
    col_off = jnp.clip(cols[None, :] - cols[:, None], -(NA_KC - 1), NA_KC - 1) + (NA_KC - 1)
    rpb_cols = rpb[:, :, col_off]
    key_mask = jnp.tile(col_mask, (1, kr))

    def row_block(r):
        rs = jnp.clip(r - kr // 2, 0, rows - kr)
        kb = lax.dynamic_slice_in_dim(k, rs, kr, axis=2).reshape(B, NA_HEADS, kr * W, NA_HEAD_DIM)
        vb = lax.dynamic_slice_in_dim(v, rs, kr, axis=2).reshape(B, NA_HEADS, kr * W, NA_HEAD_DIM)
        qr = lax.dynamic_index_in_dim(q, r, axis=2, keepdims=False)
        row_off = rs + jnp.arange(kr) - r + (NA_KR - 1)
        bias = jnp.take(rpb_cols, row_off, axis=1)
        bias = bias.transpose(0, 2, 1, 3).reshape(NA_HEADS, W, kr * W).astype(jnp.float32)
        s = jnp.einsum('bhqd,bhkd->bhqk', qr, kb).astype(jnp.float32) + bias
        s = jnp.where(key_mask, s, -jnp.inf)
        a = jax.nn.softmax(s, axis=-1)
        return jnp.einsum('bhqk,bhkd->bhqd', a.astype(vb.dtype), vb)

    o = lax.map(row_block, jnp.arange(rows))
    o = o.transpose(1, 0, 3, 2, 4).reshape(B, S, D_MODEL)
    return o @ w_o


def swiglu(h, w_gate, w_up, w_down):
    return (jax.nn.silu(h @ w_gate) * (h @ w_up)) @ w_down


def setup_inputs(seed: int = 0) -> dict:
    key = jax.random.key(seed)
    ks = jax.random.split(key, 24)
    n_a = (DEPTH + 1) // 2
    n_b = DEPTH // 2
    D, F = D_MODEL, FFN_HIDDEN
    f32 = jnp.float32

    def nrm(k, shape, scale):
        return jax.random.normal(k, shape, f32) * scale

    return {
        "x": nrm(ks[0], (BATCH, SEQ, D), 1.0),
        "p": nrm(ks[1], (DEPTH, BATCH, SEQ, PLE_DIM), 1.0),
        "norm_mix": 1.0 + nrm(ks[2], (DEPTH, D), 0.05),
        "norm_ffn": 1.0 + nrm(ks[3], (DEPTH, D), 0.05),
        "norm_ple": 1.0 + nrm(ks[4], (DEPTH, D), 0.05),
        "norm_final": 1.0 + nrm(ks[5], (D,), 0.05),
        "da_w_qkv": nrm(ks[6], (n_a, D, 3 * D), D ** -0.5),
        "da_w_o": nrm(ks[7], (n_a, D, D), D ** -0.5),
        "da_lambda_q1": nrm(ks[8], (n_a, DA_HEAD_DIM), 0.1),
        "da_lambda_k1": nrm(ks[9], (n_a, DA_HEAD_DIM), 0.1),
        "da_lambda_q2": nrm(ks[10], (n_a, DA_HEAD_DIM), 0.1),
        "da_lambda_k2": nrm(ks[11], (n_a, DA_HEAD_DIM), 0.1),
        "da_subln": 1.0 + nrm(ks[12], (n_a, 2 * DA_HEAD_DIM), 0.05),
        "na_w_qkv": nrm(ks[13], (n_b, D, 3 * D), D ** -0.5),
        "na_b_qkv": nrm(ks[14], (n_b, 3 * D), 0.02),
        "na_rpb": nrm(ks[15], (n_b, NA_HEADS, 2 * NA_KR - 1, 2 * NA_KC - 1), 0.5),
        "na_w_o": nrm(ks[16], (n_b, D, D), D ** -0.5),
        "ffn_w_gate": nrm(ks[17], (DEPTH, D, F), D ** -0.5),
        "ffn_w_up": nrm(ks[18], (DEPTH, D, F), D ** -0.5),
        "ffn_w_down": nrm(ks[19], (DEPTH, F, D), F ** -0.5),
        "ple_w_proj": nrm(ks[20], (DEPTH, PLE_DIM, D), PLE_DIM ** -0.5),
        "ple_w_gate": nrm(ks[21], (DEPTH, D, D), D ** -0.5),
    }


def reference(x, p, norm_mix, norm_ffn, norm_ple, norm_final,
              da_w_qkv, da_w_o, da_lambda_q1, da_lambda_k1, da_lambda_q2, da_lambda_k2, da_subln,
              na_w_qkv, na_b_qkv, na_rpb, na_w_o,
              ffn_w_gate, ffn_w_up, ffn_w_down, ple_w_proj, ple_w_gate):
    h = x
    for i in range(DEPTH):
        hn = rmsnorm(h, norm_mix[i])
        j = i // N_MIXERS
        if i % N_MIXERS == 0:
            mix = diff_attention(hn, da_w_qkv[j], da_w_o[j], da_lambda_q1[j], da_lambda_k1[j],
                                 da_lambda_q2[j], da_lambda_k2[j], da_subln[j], i)
        else:
            mix = neighborhood_attention(hn, na_w_qkv[j], na_b_qkv[j], na_rpb[j], na_w_o[j])
        h = h + mix
        h = h + swiglu(rmsnorm(h, norm_ffn[i]), ffn_w_gate[i], ffn_w_up[i], ffn_w_down[i])
        gate = jax.nn.sigmoid(rmsnorm(h, norm_ple[i]) @ ple_w_gate[i])
        h = h + gate * (p[i] @ ple_w_proj[i])
    return rmsnorm(h, norm_final)
```

```python
import functools
import math

import numpy as np
import jax
import jax.numpy as jnp
from jax import lax
from jax.experimental import pallas as pl
from jax.experimental.pallas import tpu as pltpu

F32 = jnp.float32
BF16 = jnp.bfloat16

D_MODEL = 1024
DEPTH = 4
PLE_DIM = 256
FFN_HIDDEN = 2816
DA_HEADS = 8
DA_HEAD_DIM = 64
NA_HEADS = 16
NA_HEAD_DIM = 64
GRID_W = 64
NA_KR = 8
NA_KC = 16
NORM_EPS = 1e-6
SUBLN_EPS = 1e-5

LANES = 128
VMEM_LIMIT = 56 * 1024 * 1024

QKV_TM = 1024
QKV_TN = 1024
POST_TM = 512
FFN_TF = 256
DA_TQ = 256
NA_QROWS = 8
NA_KROWS = 16


def _rms(x, g, eps):
    return x * lax.rsqrt(jnp.mean(x * x, axis=-1, keepdims=True) + eps) * g


def _sigmoid(x):
    return 1.0 / (1.0 + jnp.exp(-x))


def _qkv_kernel(x_ref, g_ref, w_ref, b_ref, cs_ref, o_ref, xn_ref):
    @pl.when(pl.program_id(1) == 0)
    def _():
        xn_ref[...] = _rms(x_ref[...], g_ref[...], NORM_EPS).astype(BF16)

    acc = jnp.dot(xn_ref[...], w_ref[...], preferred_element_type=F32)
    o_ref[...] = ((acc + b_ref[...]) * cs_ref[...]).astype(o_ref.dtype)


def _qkv_proj(h, g, w, b, cs):
    t, d = h.shape
    n = w.shape[1]
    return pl.pallas_call(
        _qkv_kernel,
        grid=(t // QKV_TM, n // QKV_TN),
        in_specs=[
            pl.BlockSpec((QKV_TM, d), lambda i, j: (i, 0)),
            pl.BlockSpec((1, d), lambda i, j: (0, 0)),
            pl.BlockSpec((d, QKV_TN), lambda i, j: (0, j)),
            pl.BlockSpec((1, QKV_TN), lambda i, j: (0, j)),
            pl.BlockSpec((1, QKV_TN), lambda i, j: (0, j)),
        ],
        out_specs=pl.BlockSpec((QKV_TM, QKV_TN), lambda i, j: (i, j)),
        out_shape=jax.ShapeDtypeStruct((t, n), BF16),
        scratch_shapes=[pltpu.VMEM((QKV_TM, d), BF16)],
        compiler_params=pltpu.CompilerParams(
            dimension_semantics=("parallel", "arbitrary"), vmem_limit_bytes=VMEM_LIMIT),
        name="qkv_proj",
    )(h, g, w, b, cs)


def _da_kernel(lam_init, slopes_ref, q_ref, k_ref, v_ref, lq1_ref, lk1_ref, lq2_ref, lk2_ref,
               sub_ref, o_ref):
    h = pl.program_id(1)
    i = pl.program_id(2)
    tq = q_ref.shape[1]
    s_len = k_ref.shape[1]

    q = q_ref[0]
    k = k_ref[0]
    v = v_ref[0]
    lane = lax.broadcasted_iota(jnp.int32, q.shape, 1)
    zero = jnp.zeros_like(q)
    q_maps = (jnp.where(lane < DA_HEAD_DIM, q, zero), jnp.where(lane >= DA_HEAD_DIM, q, zero))

    qpos = i * tq + lax.broadcasted_iota(jnp.int32, (tq, s_len), 0)
    kpos = lax.broadcasted_iota(jnp.int32, (tq, s_len), 1)
    bias = -slopes_ref[h] * jnp.abs(qpos - kpos).astype(F32)

    probs = []
    for qm in q_maps:
        s = lax.dot_general(qm, k, (((1,), (1,)), ((), ())), preferred_element_type=F32) + bias
        m = jnp.max(s, axis=-1, keepdims=True)
        p = jnp.exp(s - m)
        probs.append(p / jnp.sum(p, axis=-1, keepdims=True))

    lam = (jnp.exp(jnp.sum(lq1_ref[...] * lk1_ref[...], axis=-1, keepdims=True))
           - jnp.exp(jnp.sum(lq2_ref[...] * lk2_ref[...], axis=-1, keepdims=True))
           + lam_init)
    a = probs[0] - lam * probs[1]
    o = jnp.dot(a.astype(BF16), v, preferred_element_type=F32)
    o = _rms(o, sub_ref[...], SUBLN_EPS) * (1.0 - lam_init)
    o_ref[0] = o.astype(o_ref.dtype)


def _diff_attention(qkv, lq1, lk1, lq2, lk2, subln, layer_idx):
    b, s_len, _ = qkv.shape
    lam_init = 0.8 - 0.6 * math.exp(-0.3 * layer_idx)
    slopes = jnp.array([2.0 ** (-8.0 * (hh + 1) / DA_HEADS) for hh in range(DA_HEADS)], dtype=F32)
    vec = lambda a: a.reshape(1, -1).astype(F32)
    small = pl.BlockSpec((1, DA_HEAD_DIM), lambda bb, hh, ii: (0, 0))
    return pl.pallas_call(
        functools.partial(_da_kernel, lam_init),
        grid=(b, DA_HEADS, s_len // DA_TQ),
        in_specs=[
            pl.BlockSpec(memory_space=pltpu.SMEM),
            pl.BlockSpec((1, DA_TQ, LANES), lambda bb, hh, ii: (bb, ii, hh)),
            pl.BlockSpec((1, s_len, LANES), lambda bb, hh, ii: (bb, 0, DA_HEADS + hh)),
            pl.BlockSpec((1, s_len, LANES), lambda bb, hh, ii: (bb, 0, 2 * DA_HEADS + hh)),
            small, small, small, small,
            pl.BlockSpec((1, LANES), lambda bb, hh, ii: (0, 0)),
        ],
        out_specs=pl.BlockSpec((1, DA_TQ, LANES), lambda bb, hh, ii: (bb, ii, hh)),
        out_shape=jax.ShapeDtypeStruct((b, s_len, D_MODEL), BF16),
        compiler_params=pltpu.CompilerParams(
            dimension_semantics=("parallel", "parallel", "arbitrary"), vmem_limit_bytes=VMEM_LIMIT),
        name="diff_attention",
    )(slopes, qkv, qkv, qkv, vec(lq1), vec(lk1), vec(lq2), vec(lk2), vec(subln))


def _na_block_geometry(rows):
    return ((0, 0), (NA_QROWS, NA_QROWS - NA_KR // 2), (rows - NA_QROWS, rows - NA_KROWS))


def _na_bias_tables(rpb, rows):
    w = GRID_W
    kr = min(NA_KR, rows)
    qi = np.arange(NA_QROWS * w)
    kj = np.arange(NA_KROWS * w)
    cq, ck = (qi % w)[:, None], (kj % w)[None, :]
    cstart = np.clip(cq - NA_KC // 2, 0, w - NA_KC)
    col_ok = (ck >= cstart) & (ck < cstart + NA_KC)
    cidx = np.clip(ck - cq, -(NA_KC - 1), NA_KC - 1) + (NA_KC - 1)
    tables = []
    for r0, k0 in _na_block_geometry(rows):
        rq, rk = (r0 + qi // w)[:, None], (k0 + kj // w)[None, :]
        rs = np.clip(rq - kr // 2, 0, rows - kr)
        ok = (rk >= rs) & (rk < rs + kr) & col_ok
        ridx = np.clip(rk - rq + (NA_KR - 1), 0, 2 * NA_KR - 2)
        ridx, cidx_b = np.broadcast_arrays(ridx, cidx)
        vals = rpb[:, ridx, cidx_b].astype(F32)
        tables.append(jnp.where(jnp.asarray(ok)[None], vals, -jnp.inf))
    t = jnp.stack(tables, axis=1)
    t = t.reshape(NA_HEADS // 2, 2, 3, t.shape[2], t.shape[3])
    return t.transpose(0, 2, 1, 3, 4)


def _na_kernel(rows, q_ref, k_ref, v_ref, tab_ref, o_ref):
    qb = pl.program_id(2)
    nqb = rows // NA_QROWS
    kind = jnp.where(qb == 0, 0, jnp.where(qb == nqb - 1, 2, 1))
    krow0 = jnp.clip(qb * NA_QROWS - NA_KR // 2, 0, rows - NA_KROWS)
    kstart = pl.multiple_of(krow0 * GRID_W, 256)
    nk = NA_KROWS * GRID_W

    q = q_ref[0]
    kwin = k_ref[0, pl.ds(kstart, nk), :]
    vwin = v_ref[0, pl.ds(kstart, nk), :]
    lane = lax.broadcasted_iota(jnp.int32, q.shape, 1)
    zero = jnp.zeros_like(q)
    outs = []
    for hh in range(2):
        in_head = (lane >= hh * NA_HEAD_DIM) & (lane < (hh + 1) * NA_HEAD_DIM)
        qh = jnp.where(in_head, q, zero)
        s = lax.dot_general(qh, kwin, (((1,), (1,)), ((), ())), preferred_element_type=F32)
        s = s + tab_ref[0, kind, hh]
        m = jnp.max(s, axis=-1, keepdims=True)
        p = jnp.exp(s - m)
        a = p / jnp.sum(p, axis=-1, keepdims=True)
        outs.append(jnp.dot(a.astype(BF16), vwin, preferred_element_type=F32))
    o_ref[0] = jnp.where(lane < NA_HEAD_DIM, outs[0], outs[1]).astype(o_ref.dtype)


def _neighborhood_attention(qkv, rpb):
    b, s_len, _ = qkv.shape
    rows = s_len // GRID_W
    assert rows % NA_QROWS == 0 and rows >= NA_KROWS
    tables = _na_bias_tables(rpb, rows)
    nq = NA_QROWS * GRID_W
    hp = NA_HEADS // 2
    return pl.pallas_call(
        functools.partial(_na_kernel, rows),
        grid=(hp, b, rows // NA_QROWS),
        in_specs=[
            pl.BlockSpec((1, nq, LANES), lambda pp, bb, ii: (bb, ii, pp)),
            pl.BlockSpec((1, s_len, LANES), lambda pp, bb, ii: (bb, 0, hp + pp)),
            pl.BlockSpec((1, s_len, LANES), lambda pp, bb, ii: (bb, 0, 2 * hp + pp)),
            pl.BlockSpec((1,) + tables.shape[1:], lambda pp, bb, ii: (pp, 0, 0, 0, 0)),
        ],
        out_specs=pl.BlockSpec((1, nq, LANES), lambda pp, bb, ii: (bb, ii, pp)),
        out_shape=jax.ShapeDtypeStruct((b, s_len, D_MODEL), BF16),
        compiler_params=pltpu.CompilerParams(
            dimension_semantics=("parallel", "parallel", "arbitrary"), vmem_limit_bytes=VMEM_LIMIT),
        name="neighborhood_attention",
    )(qkv, qkv, qkv, tables)


def _post_kernel(final, o_ref, h_ref, p_ref, wo_ref, gf_ref, wg_ref, wu_ref, wd_ref,
                 gp_ref, wpg_ref, wpp_ref, gfin_ref, out_ref):
    h = h_ref[...] + jnp.dot(o_ref[...], wo_ref[...], preferred_element_type=F32)

    n = _rms(h, gf_ref[...], NORM_EPS).astype(BF16)
    acc = h
    for c in range(FFN_HIDDEN // FFN_TF):
        cols = slice(c * FFN_TF, (c + 1) * FFN_TF)
        gate = jnp.dot(n, wg_ref[:, cols], preferred_element_type=F32)
        up = jnp.dot(n, wu_ref[:, cols], preferred_element_type=F32)
        act = (gate * _sigmoid(gate) * up).astype(BF16)
        acc = acc + jnp.dot(act, wd_ref[cols, :], preferred_element_type=F32)
    h = acc

    n = _rms(h, gp_ref[...], NORM_EPS).astype(BF16)
    gate = _sigmoid(jnp.dot(n, wpg_ref[...], preferred_element_type=F32))
    proj = jnp.dot(p_ref[...].astype(BF16), wpp_ref[...], preferred_element_type=F32)
    h = h + gate * proj
    if final:
        h = _rms(h, gfin_ref[...], NORM_EPS)
    out_ref[...] = h


def _post_block(o, h, p, layer, wo, g_ffn, wg, wu, wd, g_ple, wpg, wpp, g_final, final):
    t, d = h.shape
    tok = lambda i: (i, 0)
    const = lambda i: (0, 0)
    resident = lambda a: pl.BlockSpec(a.shape, const, pipeline_mode=pl.Buffered(1))
    return pl.pallas_call(
        functools.partial(_post_kernel, final),
        grid=(t // POST_TM,),
        in_specs=[
            pl.BlockSpec((POST_TM, d), tok),
            pl.BlockSpec((POST_TM, d), tok),
            pl.BlockSpec((None, POST_TM, PLE_DIM), lambda i: (layer, i, 0)),
            resident(wo), resident(g_ffn), resident(wg), resident(wu), resident(wd),
            resident(g_ple), resident(wpg), resident(wpp), resident(g_final),
        ],
        out_specs=pl.BlockSpec((POST_TM, d), tok),
        out_shape=jax.ShapeDtypeStruct((t, d), F32),
        input_output_aliases={1: 0},
        compiler_params=pltpu.CompilerParams(
            dimension_semantics=("parallel",), vmem_limit_bytes=VMEM_LIMIT),
        name="post_block",
    )(o, h, p, wo, g_ffn, wg, wu, wd, g_ple, wpg, wpp, g_final)


def kernel(x, p, norm_mix, norm_ffn, norm_ple, norm_final, da_w_qkv, da_w_o, da_lambda_q1, da_lambda_k1, da_lambda_q2, da_lambda_k2, da_subln, na_w_qkv, na_b_qkv, na_rpb, na_w_o, ffn_w_gate, ffn_w_up, ffn_w_down, ple_w_proj, ple_w_gate):
    b, s_len, d = x.shape
    t = b * s_len
    row = lambda a: a.reshape(1, -1).astype(F32)
    q_scale = jnp.concatenate([jnp.full((d,), DA_HEAD_DIM ** -0.5, F32), jnp.ones((2 * d,), F32)])
    q_scale = q_scale.reshape(1, -1)
    p = p.reshape(DEPTH, t, PLE_DIM)
    h = x.reshape(t, d)
    for i in range(DEPTH):
        j = i // 2
        if i % 2 == 0:
            qkv = _qkv_proj(h, row(norm_mix[i]), da_w_qkv[j].astype(BF16),
                            jnp.zeros((1, 3 * d), F32), q_scale)
            mix = _diff_attention(qkv.reshape(b, s_len, 3 * d), da_lambda_q1[j], da_lambda_k1[j],
                                  da_lambda_q2[j], da_lambda_k2[j], da_subln[j], i)
            w_o = da_w_o[j]
        else:
            qkv = _qkv_proj(h, row(norm_mix[i]), na_w_qkv[j].astype(BF16),
                            row(na_b_qkv[j]), q_scale)
            mix = _neighborhood_attention(qkv.reshape(b, s_len, 3 * d), na_rpb[j])
            w_o = na_w_o[j]
        h = _post_block(mix.reshape(t, d), h, p, i, w_o.astype(BF16), row(norm_ffn[i]),
                        ffn_w_gate[i].astype(BF16), ffn_w_up[i].astype(BF16),
                        ffn_w_down[i].astype(BF16), row(norm_ple[i]),
                        ple_w_gate[i].astype(BF16), ple_w_proj[i].astype(BF16),
                        row(norm_final), final=(i == DEPTH - 1))
    return h.reshape(b, s_len, d)
```

```python
import functools
import math

import numpy as np
import jax
import jax.numpy as jnp
from jax import lax
from jax.experimental import pallas as pl
from jax.experimental.pallas import tpu as pltpu

F32 = jnp.float32
BF16 = jnp.bfloat16

D_MODEL = 1024
DEPTH = 4
PLE_DIM = 256
FFN_HIDDEN = 2816
DA_HEADS = 8
DA_HEAD_DIM = 64
NA_HEADS = 16
NA_HEAD_DIM = 64
GRID_W = 64
NA_KR = 8
NA_KC = 16
NORM_EPS = 1e-6
SUBLN_EPS = 1e-5

LANES = 128
VMEM_LIMIT = 56 * 1024 * 1024

QKV_TM = 1024
QKV_TN = 1024
POST_TM = 512
FFN_TF = 256
DA_TQ = 256
NA_QROWS = 8
NA_KROWS = 16


def _rms(x, g, eps):
    return x * lax.rsqrt(jnp.mean(x * x, axis=-1, keepdims=True) + eps) * g


def _sigmoid(x):
    return 1.0 / (1.0 + jnp.exp(-x))


def _qkv_kernel(x_ref, g_ref, w_ref, b_ref, cs_ref, o_ref, xn_ref):
    @pl.when(pl.program_id(1) == 0)
    def _():
        xn_ref[...] = _rms(x_ref[...], g_ref[...], NORM_EPS).astype(BF16)

    acc = jnp.dot(xn_ref[...], w_ref[...], preferred_element_type=F32)
    o_ref[...] = ((acc + b_ref[...]) * cs_ref[...]).astype(o_ref.dtype)


def _qkv_proj(h, g, w, b, cs):
    t, d = h.shape
    n = w.shape[1]
    return pl.pallas_call(
        _qkv_kernel,
        grid=(t // QKV_TM, n // QKV_TN),
        in_specs=[
            pl.BlockSpec((QKV_TM, d), lambda i, j: (i, 0)),
            pl.BlockSpec((1, d), lambda i, j: (0, 0)),
            pl.BlockSpec((d, QKV_TN), lambda i, j: (0, j)),
            pl.BlockSpec((1, QKV_TN), lambda i, j: (0, j)),
            pl.BlockSpec((1, QKV_TN), lambda i, j: (0, j)),
        ],
        out_specs=pl.BlockSpec((QKV_TM, QKV_TN), lambda i, j: (i, j)),
        out_shape=jax.ShapeDtypeStruct((t, n), BF16),
        scratch_shapes=[pltpu.VMEM((QKV_TM, d), BF16)],
        compiler_params=pltpu.CompilerParams(
            dimension_semantics=("parallel", "arbitrary"), vmem_limit_bytes=VMEM_LIMIT),
        name="qkv_proj",
    )(h, g, w, b, cs)


def _da_kernel(lam_init, slopes_ref, q_ref, k_ref, v_ref, lq1_ref, lk1_ref, lq2_ref, lk2_ref,
               sub_ref, o_ref):
    h = pl.program_id(1)
    i = pl.program_id(2)
    tq = q_ref.shape[1]
    s_len = k_ref.shape[1]

    q = q_ref[0]
    k = k_ref[0]
    v = v_ref[0]
    lane = lax.broadcasted_iota(jnp.int32, q.shape, 1)
    zero = jnp.zeros_like(q)
    q_maps = (jnp.where(lane < DA_HEAD_DIM, q, zero), jnp.where(lane >= DA_HEAD_DIM, q, zero))

    qpos = i * tq + lax.broadcasted_iota(jnp.int32, (tq, s_len), 0)
    kpos = lax.broadcasted_iota(jnp.int32, (tq, s_len), 1)
    bias = -slopes_ref[h] * jnp.abs(qpos - kpos).astype(F32)

    probs = []
    for qm in q_maps:
        s = lax.dot_general(qm, k, (((1,), (1,)), ((), ())), preferred_element_type=F32) + bias
        m = jnp.max(s, axis=-1, keepdims=True)
        p = jnp.exp(s - m)
        probs.append(p / jnp.sum(p, axis=-1, keepdims=True))

    lam = (jnp.exp(jnp.sum(lq1_ref[...] * lk1_ref[...], axis=-1, keepdims=True))
           - jnp.exp(jnp.sum(lq2_ref[...] * lk2_ref[...], axis=-1, keepdims=True))
           + lam_init)
    a = probs[0] - lam * probs[1]
    o = jnp.dot(a.astype(BF16), v, preferred_element_type=F32)
    o = _rms(o, sub_ref[...], SUBLN_EPS) * (1.0 - lam_init)
    o_ref[0] = o.astype(o_ref.dtype)


def _diff_attention(qkv, lq1, lk1, lq2, lk2, subln, layer_idx):
    b, s_len, _ = qkv.shape
    lam_init = 0.8 - 0.6 * math.exp(-0.3 * layer_idx)
    slopes = jnp.array([2.0 ** (-8.0 * (hh + 1) / DA_HEADS) for hh in range(DA_HEADS)], dtype=F32)
    vec = lambda a: a.reshape(1, -1).astype(F32)
    small = pl.BlockSpec((1, DA_HEAD_DIM), lambda bb, hh, ii: (0, 0))
    return pl.pallas_call(
        functools.partial(_da_kernel, lam_init),
        grid=(b, DA_HEADS, s_len // DA_TQ),
        in_specs=[
            pl.BlockSpec(memory_space=pltpu.SMEM),
            pl.BlockSpec((1, DA_TQ, LANES), lambda bb, hh, ii: (bb, ii, hh)),
            pl.BlockSpec((1, s_len, LANES), lambda bb, hh, ii: (bb, 0, DA_HEADS + hh)),
            pl.BlockSpec((1, s_len, LANES), lambda bb, hh, ii: (bb, 0, 2 * DA_HEADS + hh)),
            small, small, small, small,
            pl.BlockSpec((1, LANES), lambda bb, hh, ii: (0, 0)),
        ],
        out_specs=pl.BlockSpec((1, DA_TQ, LANES), lambda bb, hh, ii: (bb, ii, hh)),
        out_shape=jax.ShapeDtypeStruct((b, s_len, D_MODEL), BF16),
        compiler_params=pltpu.CompilerParams(
            dimension_semantics=("parallel", "parallel", "arbitrary"), vmem_limit_bytes=VMEM_LIMIT),
        name="diff_attention",
    )(slopes, qkv, qkv, qkv, vec(lq1), vec(lk1), vec(lq2), vec(lk2), vec(subln))


def _na_block_geometry(rows):
    return ((0, 0), (NA_QROWS, NA_QROWS - NA_KR // 2), (rows - NA_QROWS, rows - NA_KROWS))


def _na_bias_tables(rpb, rows):
    w = GRID_W
    kr = min(NA_KR, rows)
    n_dr = 2 * NA_KR - 1
    cq, ck = np.arange(w)[:, None], np.arange(w)[None, :]
    cstart = np.clip(cq - NA_KC // 2, 0, w - NA_KC)
    col_ok = (ck >= cstart) & (ck < cstart + NA_KC)
    col_off = np.clip(ck - cq, -(NA_KC - 1), NA_KC - 1) + (NA_KC - 1)
    onehot = (col_off[..., None] == np.arange(2 * NA_KC - 1)).astype(np.float32)
    tiles = jnp.einsum("hdo,qko->hdqk", rpb.astype(F32), onehot, precision=lax.Precision.HIGHEST)
    tiles = jnp.where(col_ok, tiles, -jnp.inf)
    masked = jnp.full((NA_HEADS, 1, w, w), -jnp.inf, F32)
    tiles = jnp.concatenate([tiles, masked], axis=1)
    tables = []
    for r0, k0 in _na_block_geometry(rows):
        block_rows = []
        for qi in range(NA_QROWS):
            rq = r0 + qi
            rs = min(max(rq - kr // 2, 0), rows - kr)
            picks = []
            for kj in range(NA_KROWS):
                rk = k0 + kj
                picks.append(rk - rq + (NA_KR - 1) if rs <= rk < rs + kr else n_dr)
            block_rows.append(jnp.concatenate([tiles[:, dr] for dr in picks], axis=-1))
        tables.append(jnp.concatenate(block_rows, axis=1))
    t = jnp.stack(tables, axis=1)
    t = t.reshape(NA_HEADS // 2, 2, 3, t.shape[2], t.shape[3])
    return t.transpose(0, 2, 1, 3, 4)


def _na_kernel(rows, q_ref, k_ref, v_ref, tab_ref, o_ref):
    qb = pl.program_id(2)
    nqb = rows // NA_QROWS
    kind = jnp.where(qb == 0, 0, jnp.where(qb == nqb - 1, 2, 1))
    krow0 = jnp.clip(qb * NA_QROWS - NA_KR // 2, 0, rows - NA_KROWS)
    kstart = pl.multiple_of(krow0 * GRID_W, 256)
    nk = NA_KROWS * GRID_W

    q = q_ref[0]
    kwin = k_ref[0, pl.ds(kstart, nk), :]
    vwin = v_ref[0, pl.ds(kstart, nk), :]
    lane = lax.broadcasted_iota(jnp.int32, q.shape, 1)
    zero = jnp.zeros_like(q)
    outs = []
    for hh in range(2):
        in_head = (lane >= hh * NA_HEAD_DIM) & (lane < (hh + 1) * NA_HEAD_DIM)
        qh = jnp.where(in_head, q, zero)
        s = lax.dot_general(qh, kwin, (((1,), (1,)), ((), ())), preferred_element_type=F32)
        s = s + tab_ref[0, kind, hh]
        m = jnp.max(s, axis=-1, keepdims=True)
        p = jnp.exp(s - m)
        a = p / jnp.sum(p, axis=-1, keepdims=True)
        outs.append(jnp.dot(a.astype(BF16), vwin, preferred_element_type=F32))
    o_ref[0] = jnp.where(lane < NA_HEAD_DIM, outs[0], outs[1]).astype(o_ref.dtype)


def _neighborhood_attention(qkv, rpb):
    b, s_len, _ = qkv.shape
    rows = s_len // GRID_W
    assert rows % NA_QROWS == 0 and rows >= NA_KROWS
    tables = _na_bias_tables(rpb, rows)
    nq = NA_QROWS * GRID_W
    hp = NA_HEADS // 2
    return pl.pallas_call(
        functools.partial(_na_kernel, rows),
        grid=(hp, b, rows // NA_QROWS),
        in_specs=[
            pl.BlockSpec((1, nq, LANES), lambda pp, bb, ii: (bb, ii, pp)),
            pl.BlockSpec((1, s_len, LANES), lambda pp, bb, ii: (bb, 0, hp + pp)),
            pl.BlockSpec((1, s_len, LANES), lambda pp, bb, ii: (bb, 0, 2 * hp + pp)),
            pl.BlockSpec((1,) + tables.shape[1:], lambda pp, bb, ii: (pp, 0, 0, 0, 0)),
        ],
        out_specs=pl.BlockSpec((1, nq, LANES), lambda pp, bb, ii: (bb, ii, pp)),
        out_shape=jax.ShapeDtypeStruct((b, s_len, D_MODEL), BF16),
        compiler_params=pltpu.CompilerParams(
            dimension_semantics=("parallel", "parallel", "arbitrary"), vmem_limit_bytes=VMEM_LIMIT),
        name="neighborhood_attention",
    )(qkv, qkv, qkv, tables)


def _post_kernel(final, o_ref, h_ref, p_ref, wo_ref, gf_ref, wg_ref, wu_ref, wd_ref,
                 gp_ref, wpg_ref, wpp_ref, gfin_ref, out_ref):
    h = h_ref[...] + jnp.dot(o_ref[...], wo_ref[...], preferred_element_type=F32)

    n = _rms(h, gf_ref[...], NORM_EPS).astype(BF16)
    acc = h
    for c in range(FFN_HIDDEN // FFN_TF):
        cols = slice(c * FFN_TF, (c + 1) * FFN_TF)
        gate = jnp.dot(n, wg_ref[:, cols], preferred_element_type=F32)
        up = jnp.dot(n, wu_ref[:, cols], preferred_element_type=F32)
        act = (gate * _sigmoid(gate) * up).astype(BF16)
        acc = acc + jnp.dot(act, wd_ref[cols, :], preferred_element_type=F32)
    h = acc

    n = _rms(h, gp_ref[...], NORM_EPS).astype(BF16)
    gate = _sigmoid(jnp.dot(n, wpg_ref[...], preferred_element_type=F32))
    proj = jnp.dot(p_ref[...].astype(BF16), wpp_ref[...], preferred_element_type=F32)
    h = h + gate * proj
    if final:
        h = _rms(h, gfin_ref[...], NORM_EPS)
    out_ref[...] = h


def _post_block(o, h, p, layer, wo, g_ffn, wg, wu, wd, g_ple, wpg, wpp, g_final, final):
    t, d = h.shape
    tok = lambda i: (i, 0)
    const = lambda i: (0, 0)
    resident = lambda a: pl.BlockSpec(a.shape, const, pipeline_mode=pl.Buffered(1))
    return pl.pallas_call(
        functools.partial(_post_kernel, final),
        grid=(t // POST_TM,),
        in_specs=[
            pl.BlockSpec((POST_TM, d), tok),
            pl.BlockSpec((POST_TM, d), tok),
            pl.BlockSpec((None, POST_TM, PLE_DIM), lambda i: (layer, i, 0)),
            resident(wo), resident(g_ffn), resident(wg), resident(wu), resident(wd),
            resident(g_ple), resident(wpg), resident(wpp), resident(g_final),
        ],
        out_specs=pl.BlockSpec((POST_TM, d), tok),
        out_shape=jax.ShapeDtypeStruct((t, d), F32),
        input_output_aliases={1: 0},
        compiler_params=pltpu.CompilerParams(
            dimension_semantics=("parallel",), vmem_limit_bytes=VMEM_LIMIT),
        name="post_block",
    )(o, h, p, wo, g_ffn, wg, wu, wd, g_ple, wpg, wpp, g_final)


def kernel(x, p, norm_mix, norm_ffn, norm_ple, norm_final, da_w_qkv, da_w_o, da_lambda_q1, da_lambda_k1, da_lambda_q2, da_lambda_k2, da_subln, na_w_qkv, na_b_qkv, na_rpb, na_w_o, ffn_w_gate, ffn_w_up, ffn_w_down, ple_w_proj, ple_w_gate):
    b, s_len, d = x.shape
    t = b * s_len
    row = lambda a: a.reshape(1, -1).astype(F32)
    q_scale = jnp.concatenate([jnp.full((d,), DA_HEAD_DIM ** -0.5, F32), jnp.ones((2 * d,), F32)])
    q_scale = q_scale.reshape(1, -1)
    p = p.reshape(DEPTH, t, PLE_DIM)
    h = x.reshape(t, d)
    for i in range(DEPTH):
        j = i // 2
        if i % 2 == 0:
            qkv = _qkv_proj(h, row(norm_mix[i]), da_w_qkv[j].astype(BF16),
                            jnp.zeros((1, 3 * d), F32), q_scale)
            mix = _diff_attention(qkv.reshape(b, s_len, 3 * d), da_lambda_q1[j], da_lambda_k1[j],
                                  da_lambda_q2[j], da_lambda_k2[j], da_subln[j], i)
            w_o = da_w_o[j]
        else:
            qkv = _qkv_proj(h, row(norm_mix[i]), na_w_qkv[j].astype(BF16),
                            row(na_b_qkv[j]), q_scale)
            mix = _neighborhood_attention(qkv.reshape(b, s_len, 3 * d), na_rpb[j])
            w_o = na_w_o[j]
        h = _post_block(mix.reshape(t, d), h, p, i, w_o.astype(BF16), row(norm_ffn[i]),
                        ffn_w_gate[i].astype(BF16), ffn_w_up[i].astype(BF16),
                        ffn_w_down[i].astype(BF16), row(norm_ple[i]),
                        ple_w_gate[i].astype(BF16), ple_w_proj[i].astype(BF16),
                        row(norm_final), final=(i == DEPTH - 1))
    return h.reshape(b, s_len, d)
```

```python
import functools
import math

import numpy as np
import jax
import jax.numpy as jnp
from jax import lax
from jax.experimental import pallas as pl
from jax.experimental.pallas import tpu as pltpu

F32 = jnp.float32
BF16 = jnp.bfloat16

D_MODEL = 1024
DEPTH = 4
PLE_DIM = 256
FFN_HIDDEN = 2816
DA_HEADS = 8
DA_HEAD_DIM = 64
NA_HEADS = 16
NA_HEAD_DIM = 64
GRID_W = 64
NA_KR = 8
NA_KC = 16
NORM_EPS = 1e-6
SUBLN_EPS = 1e-5

LANES = 128
MXU_WIDTH = 256
VMEM_LIMIT = 56 * 1024 * 1024
N_CB = D_MODEL // LANES

QKV_TM = 1024
QKV_TN = 1024
POST_TM = 512
FFN_TF = MXU_WIDTH
DA_TQ = MXU_WIDTH
DA_RC = 32
POS_SHIFT = 6
assert (1 << POS_SHIFT) == GRID_W == DA_HEAD_DIM
NA_SUB_ROWS = 2
NA_WIN_ROWS = NA_KR + 2
NA_UNROLL = 4


def _rms(x, g, eps):
    return x * lax.rsqrt(jnp.mean(x * x, axis=-1, keepdims=True) + eps) * g


def _sigmoid(x):
    return 1.0 / (1.0 + jnp.exp(-x))


def _dot_nt(a, b):
    return lax.dot_general(a, b, (((1,), (1,)), ((), ())), preferred_element_type=F32)


def _qkv_kernel(x_ref, g_ref, w_ref, b_ref, cs_ref, o_ref, xn_ref):
    @pl.when(pl.program_id(1) == 0)
    def _():
        xn_ref[...] = _rms(x_ref[...], g_ref[...], NORM_EPS).astype(BF16)

    acc = jnp.dot(xn_ref[...], w_ref[...], preferred_element_type=F32)
    res = ((acc + b_ref[...]) * cs_ref[...]).astype(o_ref.dtype)
    for c in range(QKV_TN // LANES):
        o_ref[c] = res[:, c * LANES:(c + 1) * LANES]


def _qkv_proj(h, g, w, b, cs):
    t, d = h.shape
    n = w.shape[1]
    cb = QKV_TN // LANES
    return pl.pallas_call(
        _qkv_kernel,
        grid=(t // QKV_TM, n // QKV_TN),
        in_specs=[
            pl.BlockSpec((QKV_TM, d), lambda i, j: (i, 0)),
            pl.BlockSpec((1, d), lambda i, j: (0, 0)),
            pl.BlockSpec((d, QKV_TN), lambda i, j: (0, j)),
            pl.BlockSpec((1, QKV_TN), lambda i, j: (0, j)),
            pl.BlockSpec((1, QKV_TN), lambda i, j: (0, j)),
        ],
        out_specs=pl.BlockSpec((cb, QKV_TM, LANES), lambda i, j: (j, i, 0)),
        out_shape=jax.ShapeDtypeStruct((n // LANES, t, LANES), BF16),
        scratch_shapes=[pltpu.VMEM((QKV_TM, d), BF16)],
        compiler_params=pltpu.CompilerParams(
            dimension_semantics=("parallel", "arbitrary"), vmem_limit_bytes=VMEM_LIMIT),
        name="qkv_proj",
    )(h, g, w, b, cs)


def _da_key_features(s_len):
    pos = np.arange(s_len)
    f = np.zeros((s_len, LANES), np.float32)
    for base in (0, DA_HEAD_DIM):
        f[:, base + 0] = pos // GRID_W
        f[:, base + 1] = pos % GRID_W
        f[:, base + 2] = 1.0
        f[:, base + 3] = 1.0
    return jnp.asarray(f, BF16)


def _da_diag_correction():
    r = np.arange(DA_TQ)
    return jnp.asarray(-2.0 * np.maximum(r[None, :] - r[:, None], 0), F32)


def _da_kernel(lam_init, nb, slopes_ref, q_ref, k_ref, v_ref, kfeat_ref, corr_ref,
               lq1_ref, lk1_ref, lq2_ref, lk2_ref, sub_ref, o_ref, kaug_ref, s_ref, p_ref):
    h = pl.program_id(1)
    i = pl.program_id(2)
    tq = DA_TQ
    slope = slopes_ref[h]

    @pl.when(i == 0)
    def _():
        k = k_ref[...]
        kf = kfeat_ref[...]
        lane_k = lax.broadcasted_iota(jnp.int32, k.shape, 1)
        kaug_ref[0] = jnp.where(lane_k < DA_HEAD_DIM, k, kf)
        kaug_ref[1] = jnp.where(lane_k >= DA_HEAD_DIM, k, kf)

    q = q_ref[...]
    lane = lax.broadcasted_iota(jnp.int32, q.shape, 1)
    fidx = jnp.bitwise_and(lane, DA_HEAD_DIM - 1)
    pos = i * tq + lax.broadcasted_iota(jnp.int32, q.shape, 0)
    pos_hi = jnp.right_shift(pos, POS_SHIFT).astype(F32)
    pos_lo = jnp.bitwise_and(pos, GRID_W - 1).astype(F32)
    feat = jnp.where(fidx == 0, GRID_W * slope,
                     jnp.where(fidx == 1, slope,
                               jnp.where(fidx == 2, -GRID_W * slope * pos_hi,
                                         jnp.where(fidx == 3, -slope * pos_lo, 0.0))))
    feat_l = feat.astype(BF16)
    feat_r = (-feat).astype(BF16)
    corr = slope * corr_ref[...]

    outs = []
    for m in range(2):
        is_data = (lane < DA_HEAD_DIM) if m == 0 else (lane >= DA_HEAD_DIM)
        q_left = jnp.where(is_data, q, feat_l)
        q_right = jnp.where(is_data, q, feat_r)
        for j in range(nb):
            qs = jnp.where(j <= i, q_left, q_right)
            s_ref[m, j] = _dot_nt(qs, kaug_ref[m, j * tq:(j + 1) * tq, :])
        s_ref[m, i] = s_ref[m, i] + corr

        l_parts = []
        for rc in range(tq // DA_RC):
            rows = slice(rc * DA_RC, (rc + 1) * DA_RC)
            m_el = s_ref[m, 0, rows, :]
            for j in range(1, nb):
                m_el = jnp.maximum(m_el, s_ref[m, j, rows, :])
            mb = jnp.broadcast_to(jnp.max(m_el, axis=-1, keepdims=True), m_el.shape)
            l_el = jnp.zeros_like(m_el)
            for j in range(nb):
                p = jnp.exp(s_ref[m, j, rows, :] - mb)
                l_el = l_el + p
                p_ref[m, rows, j * tq:(j + 1) * tq] = p.astype(BF16)
            l_parts.append(jnp.sum(l_el, axis=-1, keepdims=True))
        l = jnp.concatenate(l_parts, axis=0)
        pv = jnp.dot(p_ref[m], v_ref[...], preferred_element_type=F32)
        outs.append(pv / l)

    lam = (jnp.exp(jnp.sum(lq1_ref[...] * lk1_ref[...], axis=-1, keepdims=True))
           - jnp.exp(jnp.sum(lq2_ref[...] * lk2_ref[...], axis=-1, keepdims=True))
           + lam_init)
    o = outs[0] - lam * outs[1]
    o = _rms(o, sub_ref[...], SUBLN_EPS) * (1.0 - lam_init)
    o_ref[...] = o.astype(o_ref.dtype)


def _diff_attention(qkv, batch, lq1, lk1, lq2, lk2, subln, layer_idx):
    t = qkv.shape[1]
    s_len = t // batch
    nb = s_len // DA_TQ
    lam_init = 0.8 - 0.6 * math.exp(-0.3 * layer_idx)
    slopes = [2.0 ** (-8.0 * (hh + 1) / DA_HEADS) for hh in range(DA_HEADS)]
    for sl in slopes:
        assert float(np.asarray(sl * GRID_W * (GRID_W - 1), dtype=BF16)) == sl * GRID_W * (GRID_W - 1)
    assert s_len <= GRID_W * GRID_W and s_len % DA_TQ == 0
    vec = lambda a: a.reshape(1, -1).astype(F32)
    small = pl.BlockSpec((1, DA_HEAD_DIM), lambda bb, hh, ii: (0, 0))
    return pl.pallas_call(
        functools.partial(_da_kernel, lam_init, nb),
        grid=(batch, DA_HEADS, nb),
        in_specs=[
            pl.BlockSpec(memory_space=pltpu.SMEM),
            pl.BlockSpec((None, DA_TQ, LANES), lambda bb, hh, ii: (hh, bb * nb + ii, 0)),
            pl.BlockSpec((None, s_len, LANES), lambda bb, hh, ii: (DA_HEADS + hh, bb, 0)),
            pl.BlockSpec((None, s_len, LANES), lambda bb, hh, ii: (2 * DA_HEADS + hh, bb, 0)),
            pl.BlockSpec((s_len, LANES), lambda bb, hh, ii: (0, 0)),
            pl.BlockSpec((DA_TQ, DA_TQ), lambda bb, hh, ii: (0, 0)),
            small, small, small, small,
            pl.BlockSpec((1, LANES), lambda bb, hh, ii: (0, 0)),
        ],
        out_specs=pl.BlockSpec((None, DA_TQ, LANES), lambda bb, hh, ii: (hh, bb * nb + ii, 0)),
        out_shape=jax.ShapeDtypeStruct((DA_HEADS, t, LANES), BF16),
        scratch_shapes=[
            pltpu.VMEM((2, s_len, LANES), BF16),
            pltpu.VMEM((2, nb, DA_TQ, DA_TQ), F32),
            pltpu.VMEM((2, DA_TQ, s_len), BF16),
        ],
        compiler_params=pltpu.CompilerParams(
            dimension_semantics=("parallel", "parallel", "arbitrary"), vmem_limit_bytes=VMEM_LIMIT),
        name="diff_attention",
    )(jnp.asarray(slopes, F32), qkv, qkv, qkv, _da_key_features(s_len), _da_diag_correction(),
      vec(lq1), vec(lk1), vec(lq2), vec(lk2), vec(subln))


def _na_sub_geometry(rows):
    last = rows - NA_WIN_ROWS
    return ((0, 0), (NA_SUB_ROWS, 0), (2 * NA_SUB_ROWS, 0),
            (rows - 2 * NA_SUB_ROWS, last), (rows - NA_SUB_ROWS, last))


def _na_bias_tables(rpb, rows):
    w = GRID_W
    kr = min(NA_KR, rows)
    n_dr = 2 * NA_KR - 1
    cq, ck = np.arange(w)[:, None], np.arange(w)[None, :]
    cstart = np.clip(cq - NA_KC // 2, 0, w - NA_KC)
    col_ok = (ck >= cstart) & (ck < cstart + NA_KC)
    col_off = np.clip(ck - cq, -(NA_KC - 1), NA_KC - 1) + (NA_KC - 1)
    onehot = (col_off[..., None] == np.arange(2 * NA_KC - 1)).astype(np.float32)
    tiles = jnp.einsum("hdo,qko->hdqk", rpb.astype(F32), onehot, precision=lax.Precision.HIGHEST)
    tiles = jnp.where(col_ok, tiles, -jnp.inf)
    masked = jnp.full((NA_HEADS, 1, w, w), -jnp.inf, F32)
    tiles = jnp.concatenate([tiles, masked], axis=1)
    tables = []
    for rq0, ws in _na_sub_geometry(rows):
        block_rows = []
        for qi in range(NA_SUB_ROWS):
            rq = rq0 + qi
            rs = min(max(rq - kr // 2, 0), rows - kr)
            picks = []
            for kj in range(NA_WIN_ROWS):
                rk = ws + kj
                picks.append(rk - rq + (NA_KR - 1) if rs <= rk < rs + kr else n_dr)
            block_rows.append(jnp.concatenate([tiles[:, dr] for dr in picks], axis=-1))
        tables.append(jnp.concatenate(block_rows, axis=1))
    t = jnp.stack(tables, axis=1)
    nq, nk = t.shape[2], t.shape[3]
    t = t.reshape(NA_HEADS // 2, 2, 5, nq, nk).transpose(0, 2, 1, 3, 4)
    return t.reshape(NA_HEADS // 2, 5, 2 * nq, nk)


def _na_kernel(rows, q_ref, k_ref, v_ref, tab_ref, o_ref):
    n_sub = rows // NA_SUB_ROWS
    nq = NA_SUB_ROWS * GRID_W
    nk = NA_WIN_ROWS * GRID_W

    def body(g, carry):
        ws = jnp.clip(NA_SUB_ROWS * g - NA_KR // 2, 0, rows - NA_WIN_ROWS)
        kind = jnp.where(g < 2, g, jnp.where(g >= n_sub - 2, g - (n_sub - 5), 2))
        q0 = pl.multiple_of(g * nq, nq)
        k0 = pl.multiple_of(ws * GRID_W, LANES)
        qg = q_ref[pl.ds(q0, nq), :]
        lane = lax.broadcasted_iota(jnp.int32, qg.shape, 1)
        zero = jnp.zeros_like(qg)
        q2 = jnp.concatenate([jnp.where(lane < NA_HEAD_DIM, qg, zero),
                              jnp.where(lane >= NA_HEAD_DIM, qg, zero)], axis=0)
        kw = k_ref[pl.ds(k0, nk), :]
        vw = v_ref[pl.ds(k0, nk), :]
        s = _dot_nt(q2, kw) + tab_ref[0, kind]
        m = jnp.max(s, axis=-1, keepdims=True)
        p = jnp.exp(s - m)
        l = jnp.sum(p, axis=-1, keepdims=True)
        pv = jnp.dot(p.astype(BF16), vw, preferred_element_type=F32) / l
        o_ref[pl.ds(q0, nq), :] = jnp.where(lane < NA_HEAD_DIM, pv[:nq], pv[nq:]).astype(o_ref.dtype)
        return carry

    lax.fori_loop(0, n_sub, body, 0, unroll=NA_UNROLL)


def _neighborhood_attention(qkv, batch, rpb):
    t = qkv.shape[1]
    s_len = t // batch
    rows = s_len // GRID_W
    assert rows % NA_SUB_ROWS == 0 and rows >= NA_WIN_ROWS + 2 * NA_SUB_ROWS and NA_KR % 2 == 0
    tables = _na_bias_tables(rpb, rows)
    hp = NA_HEADS // 2
    slab = lambda part: pl.BlockSpec((None, s_len, LANES), lambda pp, bb: (part * hp + pp, bb, 0))
    return pl.pallas_call(
        functools.partial(_na_kernel, rows),
        grid=(hp, batch),
        in_specs=[slab(0), slab(1), slab(2),
                  pl.BlockSpec((1,) + tables.shape[1:], lambda pp, bb: (pp, 0, 0, 0))],
        out_specs=slab(0),
        out_shape=jax.ShapeDtypeStruct((hp, t, LANES), BF16),
        compiler_params=pltpu.CompilerParams(
            dimension_semantics=("parallel", "arbitrary"), vmem_limit_bytes=VMEM_LIMIT),
        name="neighborhood_attention",
    )(qkv, qkv, qkv, tables)


def _post_kernel(final, o_ref, h_ref, p_ref, wo_ref, gf_ref, wg_ref, wu_ref, wd_ref,
                 gp_ref, wpg_ref, wpp_ref, gfin_ref, out_ref):
    o = jnp.concatenate([o_ref[c] for c in range(N_CB)], axis=-1)
    h = h_ref[...] + jnp.dot(o, wo_ref[...], preferred_element_type=F32)

    n = _rms(h, gf_ref[...], NORM_EPS).astype(BF16)
    acc = h
    for c in range(FFN_HIDDEN // FFN_TF):
        cols = slice(c * FFN_TF, (c + 1) * FFN_TF)
        gate = jnp.dot(n, wg_ref[:, cols], preferred_element_type=F32)
        up = jnp.dot(n, wu_ref[:, cols], preferred_element_type=F32)
        act = (gate * _sigmoid(gate) * up).astype(BF16)
        acc = acc + jnp.dot(act, wd_ref[cols, :], preferred_element_type=F32)
    h = acc

    n = _rms(h, gp_ref[...], NORM_EPS).astype(BF16)
    gate = _sigmoid(jnp.dot(n, wpg_ref[...], preferred_element_type=F32))
    proj = jnp.dot(p_ref[...].astype(BF16), wpp_ref[...], preferred_element_type=F32)
    h = h + gate * proj
    if final:
        h = _rms(h, gfin_ref[...], NORM_EPS)
    out_ref[...] = h


def _post_block(o, h, p, layer, wo, g_ffn, wg, wu, wd, g_ple, wpg, wpp, g_final, final):
    t, d = h.shape
    tok = lambda i: (i, 0)
    const = lambda i: (0, 0)
    resident = lambda a: pl.BlockSpec(a.shape, const, pipeline_mode=pl.Buffered(1))
    return pl.pallas_call(
        functools.partial(_post_kernel, final),
        grid=(t // POST_TM,),
        in_specs=[
            pl.BlockSpec((N_CB, POST_TM, LANES), lambda i: (0, i, 0)),
            pl.BlockSpec((POST_TM, d), tok),
            pl.BlockSpec((None, POST_TM, PLE_DIM), lambda i: (layer, i, 0)),
            resident(wo), resident(g_ffn), resident(wg), resident(wu), resident(wd),
            resident(g_ple), resident(wpg), resident(wpp), resident(g_final),
        ],
        out_specs=pl.BlockSpec((POST_TM, d), tok),
        out_shape=jax.ShapeDtypeStruct((t, d), F32),
        input_output_aliases={1: 0} if layer > 0 else {},
        compiler_params=pltpu.CompilerParams(
            dimension_semantics=("parallel",), vmem_limit_bytes=VMEM_LIMIT),
        name="post_block",
    )(o, h, p, wo, g_ffn, wg, wu, wd, g_ple, wpg, wpp, g_final)


def kernel(x, p, norm_mix, norm_ffn, norm_ple, norm_final, da_w_qkv, da_w_o, da_lambda_q1, da_lambda_k1, da_lambda_q2, da_lambda_k2, da_subln, na_w_qkv, na_b_qkv, na_rpb, na_w_o, ffn_w_gate, ffn_w_up, ffn_w_down, ple_w_proj, ple_w_gate):
    b, s_len, d = x.shape
    t = b * s_len
    row = lambda a: a.reshape(1, -1).astype(F32)
    q_scale = jnp.concatenate([jnp.full((d,), DA_HEAD_DIM ** -0.5, F32), jnp.ones((2 * d,), F32)])
    q_scale = q_scale.reshape(1, -1)
    p = p.reshape(DEPTH, t, PLE_DIM)
    h = x.reshape(t, d)
    for i in range(DEPTH):
        j = i // 2
        if i % 2 == 0:
            qkv = _qkv_proj(h, row(norm_mix[i]), da_w_qkv[j].astype(BF16),
                            jnp.zeros((1, 3 * d), F32), q_scale)
            mix = _diff_attention(qkv, b, da_lambda_q1[j], da_lambda_k1[j],
                                  da_lambda_q2[j], da_lambda_k2[j], da_subln[j], i)
            w_o = da_w_o[j]
        else:
            qkv = _qkv_proj(h, row(norm_mix[i]), na_w_qkv[j].astype(BF16),
                            row(na_b_qkv[j]), q_scale)
            mix = _neighborhood_attention(qkv, b, na_rpb[j])
            w_o = na_w_o[j]
        h = _post_block(mix, h, p, i, w_o.astype(BF16), row(norm_ffn[i]),
                        ffn_w_gate[i].astype(BF16), ffn_w_up[i].astype(BF16),
                        ffn_w_down[i].astype(BF16), row(norm_ple[i]),
                        ple_w_gate[i].astype(BF16), ple_w_proj[i].astype(BF16),
                        row(norm_final), final=(i == DEPTH - 1))
    return h.reshape(b, s_len, d)
```

```python
import functools
import math

import numpy as np
import jax
import jax.numpy as jnp
from jax import lax
from jax.experimental import pallas as pl
from jax.experimental.pallas import tpu as pltpu

F32 = jnp.float32
BF16 = jnp.bfloat16

D_MODEL = 1024
DEPTH = 4
PLE_DIM = 256
FFN_HIDDEN = 2816
DA_HEADS = 8
DA_HEAD_DIM = 64
NA_HEADS = 16
NA_HEAD_DIM = 64
GRID_W = 64
NA_KR = 8
NA_KC = 16
NORM_EPS = 1e-6
SUBLN_EPS = 1e-5

LANES = 128
MXU_WIDTH = 256
VMEM_LIMIT = 56 * 1024 * 1024
N_CB = D_MODEL // LANES

QKV_TM = 1024
QKV_TN = 1024
POST_TM = 512
FFN_TF = MXU_WIDTH
DA_TQ = MXU_WIDTH
DA_RC = 32
POS_SHIFT = 6
assert (1 << POS_SHIFT) == GRID_W == DA_HEAD_DIM
NA_SUB_ROWS = 2
NA_WIN_ROWS = NA_KR + 2
NA_UNROLL = 4


def _rms(x, g, eps):
    return x * lax.rsqrt(jnp.mean(x * x, axis=-1, keepdims=True) + eps) * g


def _sigmoid(x):
    return 1.0 / (1.0 + jnp.exp(-x))


def _dot_nt(a, b):
    return lax.dot_general(a, b, (((1,), (1,)), ((), ())), preferred_element_type=F32)


def _qkv_kernel(x_ref, g_ref, w_ref, b_ref, cs_ref, o_ref, xn_ref):
    @pl.when(pl.program_id(1) == 0)
    def _():
        xn_ref[...] = _rms(x_ref[...], g_ref[...], NORM_EPS).astype(BF16)

    acc = jnp.dot(xn_ref[...], w_ref[...], preferred_element_type=F32)
    res = ((acc + b_ref[...]) * cs_ref[...]).astype(o_ref.dtype)
    for c in range(QKV_TN // LANES):
        o_ref[c] = res[:, c * LANES:(c + 1) * LANES]


def _qkv_proj(h, g, w, b, cs):
    t, d = h.shape
    n = w.shape[1]
    cb = QKV_TN // LANES
    return pl.pallas_call(
        _qkv_kernel,
        grid=(t // QKV_TM, n // QKV_TN),
        in_specs=[
            pl.BlockSpec((QKV_TM, d), lambda i, j: (i, 0)),
            pl.BlockSpec((1, d), lambda i, j: (0, 0)),
            pl.BlockSpec((d, QKV_TN), lambda i, j: (0, j)),
            pl.BlockSpec((1, QKV_TN), lambda i, j: (0, j)),
            pl.BlockSpec((1, QKV_TN), lambda i, j: (0, j)),
        ],
        out_specs=pl.BlockSpec((cb, QKV_TM, LANES), lambda i, j: (j, i, 0)),
        out_shape=jax.ShapeDtypeStruct((n // LANES, t, LANES), BF16),
        scratch_shapes=[pltpu.VMEM((QKV_TM, d), BF16)],
        compiler_params=pltpu.CompilerParams(
            dimension_semantics=("parallel", "arbitrary"), vmem_limit_bytes=VMEM_LIMIT),
        name="qkv_proj",
    )(h, g, w, b, cs)


def _da_key_features(s_len):
    pos = np.arange(s_len)
    f = np.zeros((s_len, LANES), np.float32)
    for base in (0, DA_HEAD_DIM):
        f[:, base + 0] = pos // GRID_W
        f[:, base + 1] = pos % GRID_W
        f[:, base + 2] = 1.0
        f[:, base + 3] = 1.0
    return jnp.asarray(f, BF16)


def _da_diag_correction():
    r = np.arange(DA_TQ)
    return jnp.asarray(-2.0 * np.maximum(r[None, :] - r[:, None], 0), F32)


def _da_kernel(lam_init, nb, slopes_ref, qc_ref, qn_ref, k_ref, v_ref, kfeat_ref, corr_ref,
               lq1_ref, lk1_ref, lq2_ref, lk2_ref, sub_ref, o_ref,
               kaug_ref, vaug_ref, qv_ref, sa_ref, sb_ref, pa_ref, pb_ref):
    h = pl.program_id(1)
    t = pl.program_id(2)
    tq = DA_TQ
    slope = slopes_ref[h]
    corr = slope * corr_ref[...]

    def scores(q, i_tile, slot, s_ref):
        lane = lax.broadcasted_iota(jnp.int32, q.shape, 1)
        fidx = jnp.bitwise_and(lane, DA_HEAD_DIM - 1)
        pos = i_tile * tq + lax.broadcasted_iota(jnp.int32, q.shape, 0)
        pos_hi = jnp.right_shift(pos, POS_SHIFT).astype(F32)
        pos_lo = jnp.bitwise_and(pos, GRID_W - 1).astype(F32)
        feat = jnp.where(fidx == 0, GRID_W * slope,
                         jnp.where(fidx == 1, slope,
                                   jnp.where(fidx == 2, -GRID_W * slope * pos_hi,
                                             jnp.where(fidx == 3, -slope * pos_lo, 0.0))))
        feat_l = feat.astype(BF16)
        feat_r = (-feat).astype(BF16)
        for m in range(2):
            is_data = (lane < DA_HEAD_DIM) if m == 0 else (lane >= DA_HEAD_DIM)
            qv_ref[slot, m, 0] = jnp.where(is_data, q, feat_l)
            qv_ref[slot, m, 1] = jnp.where(is_data, q, feat_r)

        def block(m, j):
            side = jnp.where(j > i_tile, 1, 0)
            s_ref[m, j] = _dot_nt(qv_ref[slot, m, side], kaug_ref[m, j * tq:(j + 1) * tq, :])
            if j == nb - 1:
                s_ref[m, i_tile] = s_ref[m, i_tile] + corr

        return [functools.partial(block, m, j) for m in range(2) for j in range(nb)]

    def softmax_chunk(s_ref, p_ref, m, rc):
        rows = slice(rc * DA_RC, (rc + 1) * DA_RC)
        m_el = s_ref[m, 0, rows, :]
        for j in range(1, nb):
            m_el = jnp.maximum(m_el, s_ref[m, j, rows, :])
        mb = jnp.broadcast_to(jnp.max(m_el, axis=-1, keepdims=True), m_el.shape)
        for j in range(nb):
            p_ref[m, rows, j * tq:(j + 1) * tq] = jnp.exp(s_ref[m, j, rows, :] - mb).astype(BF16)

    def softmax_pv(s_ref, p_ref, mxu_tasks):
        n_chunks = tq // DA_RC
        per_chunk = len(mxu_tasks) // (2 * n_chunks)
        outs = []
        for m in range(2):
            for rc in range(n_chunks):
                softmax_chunk(s_ref, p_ref, m, rc)
                for _ in range(per_chunk):
                    mxu_tasks.pop(0)()
            res = jnp.dot(p_ref[m], vaug_ref[...], preferred_element_type=F32)
            outs.append(res[:, :LANES] / res[:, LANES:])
        while mxu_tasks:
            mxu_tasks.pop(0)()
        lam = (jnp.exp(jnp.sum(lq1_ref[...] * lk1_ref[...], axis=-1, keepdims=True))
               - jnp.exp(jnp.sum(lq2_ref[...] * lk2_ref[...], axis=-1, keepdims=True))
               + lam_init)
        o = outs[0] - lam * outs[1]
        o = _rms(o, sub_ref[...], SUBLN_EPS) * (1.0 - lam_init)
        return o.astype(o_ref.dtype)

    @pl.when(t == 0)
    def _():
        k = k_ref[...]
        kf = kfeat_ref[...]
        lane_k = lax.broadcasted_iota(jnp.int32, k.shape, 1)
        kaug_ref[0] = jnp.where(lane_k < DA_HEAD_DIM, k, kf)
        kaug_ref[1] = jnp.where(lane_k >= DA_HEAD_DIM, k, kf)
        vaug_ref[:, :LANES] = v_ref[...]
        vaug_ref[:, LANES:] = jnp.ones(v_ref.shape, BF16)
        for task in scores(qc_ref[:tq, :], 2 * t, 0, sa_ref):
            task()

    tasks_b = scores(qc_ref[tq:, :], 2 * t + 1, 1, sb_ref)
    o_ref[:tq, :] = softmax_pv(sa_ref, pa_ref, tasks_b)
    tasks_n = scores(qn_ref[...], jnp.minimum(2 * t + 2, nb - 1), 2, sa_ref)
    o_ref[tq:, :] = softmax_pv(sb_ref, pb_ref, tasks_n)


def _diff_attention(qkv, batch, lq1, lk1, lq2, lk2, subln, layer_idx):
    t = qkv.shape[1]
    s_len = t // batch
    nb = s_len // DA_TQ
    lam_init = 0.8 - 0.6 * math.exp(-0.3 * layer_idx)
    slopes = [2.0 ** (-8.0 * (hh + 1) / DA_HEADS) for hh in range(DA_HEADS)]
    for sl in slopes:
        assert float(np.asarray(sl * GRID_W * (GRID_W - 1), dtype=BF16)) == sl * GRID_W * (GRID_W - 1)
    assert s_len <= GRID_W * GRID_W and nb % 2 == 0
    steps = nb // 2
    vec = lambda a: a.reshape(1, -1).astype(F32)
    small = pl.BlockSpec((1, DA_HEAD_DIM), lambda bb, hh, ii: (0, 0))
    return pl.pallas_call(
        functools.partial(_da_kernel, lam_init, nb),
        grid=(batch, DA_HEADS, steps),
        in_specs=[
            pl.BlockSpec(memory_space=pltpu.SMEM),
            pl.BlockSpec((None, 2 * DA_TQ, LANES), lambda bb, hh, ii: (hh, bb * steps + ii, 0)),
            pl.BlockSpec((None, DA_TQ, LANES),
                         lambda bb, hh, ii: (hh, bb * nb + jnp.minimum(2 * ii + 2, nb - 1), 0)),
            pl.BlockSpec((None, s_len, LANES), lambda bb, hh, ii: (DA_HEADS + hh, bb, 0)),
            pl.BlockSpec((None, s_len, LANES), lambda bb, hh, ii: (2 * DA_HEADS + hh, bb, 0)),
            pl.BlockSpec((s_len, LANES), lambda bb, hh, ii: (0, 0)),
            pl.BlockSpec((DA_TQ, DA_TQ), lambda bb, hh, ii: (0, 0)),
            small, small, small, small,
            pl.BlockSpec((1, LANES), lambda bb, hh, ii: (0, 0)),
        ],
        out_specs=pl.BlockSpec((None, 2 * DA_TQ, LANES), lambda bb, hh, ii: (hh, bb * steps + ii, 0)),
        out_shape=jax.ShapeDtypeStruct((DA_HEADS, t, LANES), BF16),
        scratch_shapes=[
            pltpu.VMEM((2, s_len, LANES), BF16),
            pltpu.VMEM((s_len, 2 * LANES), BF16),
            pltpu.VMEM((3, 2, 2, DA_TQ, LANES), BF16),
            pltpu.VMEM((2, nb, DA_TQ, DA_TQ), F32),
            pltpu.VMEM((2, nb, DA_TQ, DA_TQ), F32),
            pltpu.VMEM((2, DA_TQ, s_len), BF16),
            pltpu.VMEM((2, DA_TQ, s_len), BF16),
        ],
        compiler_params=pltpu.CompilerParams(
            dimension_semantics=("parallel", "parallel", "arbitrary"), vmem_limit_bytes=VMEM_LIMIT),
        name="diff_attention",
    )(jnp.asarray(slopes, F32), qkv, qkv, qkv, qkv, _da_key_features(s_len), _da_diag_correction(),
      vec(lq1), vec(lk1), vec(lq2), vec(lk2), vec(subln))


def _na_sub_geometry(rows):
    last = rows - NA_WIN_ROWS
    return ((0, 0), (NA_SUB_ROWS, 0), (2 * NA_SUB_ROWS, 0),
            (rows - 2 * NA_SUB_ROWS, last), (rows - NA_SUB_ROWS, last))


def _na_bias_tables(rpb, rows):
    w = GRID_W
    kr = min(NA_KR, rows)
    n_dr = 2 * NA_KR - 1
    cq, ck = np.arange(w)[:, None], np.arange(w)[None, :]
    cstart = np.clip(cq - NA_KC // 2, 0, w - NA_KC)
    col_ok = (ck >= cstart) & (ck < cstart + NA_KC)
    col_off = np.clip(ck - cq, -(NA_KC - 1), NA_KC - 1) + (NA_KC - 1)
    onehot = (col_off[..., None] == np.arange(2 * NA_KC - 1)).astype(np.float32)
    tiles = jnp.einsum("hdo,qko->hdqk", rpb.astype(F32), onehot, precision=lax.Precision.HIGHEST)
    tiles = jnp.where(col_ok, tiles, -jnp.inf)
    masked = jnp.full((NA_HEADS, 1, w, w), -jnp.inf, F32)
    tiles = jnp.concatenate([tiles, masked], axis=1)
    tables = []
    for rq0, ws in _na_sub_geometry(rows):
        block_rows = []
        for qi in range(NA_SUB_ROWS):
            rq = rq0 + qi
            rs = min(max(rq - kr // 2, 0), rows - kr)
            picks = []
            for kj in range(NA_WIN_ROWS):
                rk = ws + kj
                picks.append(rk - rq + (NA_KR - 1) if rs <= rk < rs + kr else n_dr)
            block_rows.append(jnp.concatenate([tiles[:, dr] for dr in picks], axis=-1))
        tables.append(jnp.concatenate(block_rows, axis=1))
    t = jnp.stack(tables, axis=1)
    nq, nk = t.shape[2], t.shape[3]
    t = t.reshape(NA_HEADS // 2, 2, 5, nq, nk).transpose(0, 2, 1, 3, 4)
    return t.reshape(NA_HEADS // 2, 5, 2 * nq, nk)


def _na_kernel(rows, q_ref, k_ref, v_ref, tab_ref, o_ref):
    n_sub = rows // NA_SUB_ROWS
    nq = NA_SUB_ROWS * GRID_W
    nk = NA_WIN_ROWS * GRID_W

    def body(g, carry):
        ws = jnp.clip(NA_SUB_ROWS * g - NA_KR // 2, 0, rows - NA_WIN_ROWS)
        kind = jnp.where(g < 2, g, jnp.where(g >= n_sub - 2, g - (n_sub - 5), 2))
        q0 = pl.multiple_of(g * nq, nq)
        k0 = pl.multiple_of(ws * GRID_W, LANES)
        qg = q_ref[pl.ds(q0, nq), :]
        lane = lax.broadcasted_iota(jnp.int32, qg.shape, 1)
        zero = jnp.zeros_like(qg)
        q2 = jnp.concatenate([jnp.where(lane < NA_HEAD_DIM, qg, zero),
                              jnp.where(lane >= NA_HEAD_DIM, qg, zero)], axis=0)
        kw = k_ref[pl.ds(k0, nk), :]
        vw = v_ref[pl.ds(k0, nk), :]
        s = _dot_nt(q2, kw) + tab_ref[0, kind]
        m = jnp.max(s, axis=-1, keepdims=True)
        p = jnp.exp(s - m)
        l = jnp.sum(p, axis=-1, keepdims=True)
        pv = jnp.dot(p.astype(BF16), vw, preferred_element_type=F32) / l
        o_ref[pl.ds(q0, nq), :] = jnp.where(lane < NA_HEAD_DIM, pv[:nq], pv[nq:]).astype(o_ref.dtype)
        return carry

    lax.fori_loop(0, n_sub, body, 0, unroll=NA_UNROLL)


def _neighborhood_attention(qkv, batch, rpb):
    t = qkv.shape[1]
    s_len = t // batch
    rows = s_len // GRID_W
    assert rows % NA_SUB_ROWS == 0 and rows >= NA_WIN_ROWS + 2 * NA_SUB_ROWS and NA_KR % 2 == 0
    tables = _na_bias_tables(rpb, rows)
    hp = NA_HEADS // 2
    slab = lambda part: pl.BlockSpec((None, s_len, LANES), lambda pp, bb: (part * hp + pp, bb, 0))
    return pl.pallas_call(
        functools.partial(_na_kernel, rows),
        grid=(hp, batch),
        in_specs=[slab(0), slab(1), slab(2),
                  pl.BlockSpec((1,) + tables.shape[1:], lambda pp, bb: (pp, 0, 0, 0))],
        out_specs=slab(0),
        out_shape=jax.ShapeDtypeStruct((hp, t, LANES), BF16),
        compiler_params=pltpu.CompilerParams(
            dimension_semantics=("parallel", "arbitrary"), vmem_limit_bytes=VMEM_LIMIT),
        name="neighborhood_attention",
    )(qkv, qkv, qkv, tables)


def _post_kernel(final, o_ref, h_ref, p_ref, wo_ref, gf_ref, wg_ref, wu_ref, wd_ref,
                 gp_ref, wpg_ref, wpp_ref, gfin_ref, out_ref):
    o = jnp.concatenate([o_ref[c] for c in range(N_CB)], axis=-1)
    h = h_ref[...] + jnp.dot(o, wo_ref[...], preferred_element_type=F32)

    n = _rms(h, gf_ref[...], NORM_EPS).astype(BF16)
    acc = h
    for c in range(FFN_HIDDEN // FFN_TF):
        cols = slice(c * FFN_TF, (c + 1) * FFN_TF)
        gate = jnp.dot(n, wg_ref[:, cols], preferred_element_type=F32)
        up = jnp.dot(n, wu_ref[:, cols], preferred_element_type=F32)
        act = (gate * _sigmoid(gate) * up).astype(BF16)
        acc = acc + jnp.dot(act, wd_ref[cols, :], preferred_element_type=F32)
    h = acc

    n = _rms(h, gp_ref[...], NORM_EPS).astype(BF16)
    gate = _sigmoid(jnp.dot(n, wpg_ref[...], preferred_element_type=F32))
    proj = jnp.dot(p_ref[...].astype(BF16), wpp_ref[...], preferred_element_type=F32)
    h = h + gate * proj
    if final:
        h = _rms(h, gfin_ref[...], NORM_EPS)
    out_ref[...] = h


def _post_block(o, h, p, layer, wo, g_ffn, wg, wu, wd, g_ple, wpg, wpp, g_final, final):
    t, d = h.shape
    tok = lambda i: (i, 0)
    const = lambda i: (0, 0)
    resident = lambda a: pl.BlockSpec(a.shape, const, pipeline_mode=pl.Buffered(1))
    return pl.pallas_call(
        functools.partial(_post_kernel, final),
        grid=(t // POST_TM,),
        in_specs=[
            pl.BlockSpec((N_CB, POST_TM, LANES), lambda i: (0, i, 0)),
            pl.BlockSpec((POST_TM, d), tok),
            pl.BlockSpec((None, POST_TM, PLE_DIM), lambda i: (layer, i, 0)),
            resident(wo), resident(g_ffn), resident(wg), resident(wu), resident(wd),
            resident(g_ple), resident(wpg), resident(wpp), resident(g_final),
        ],
        out_specs=pl.BlockSpec((POST_TM, d), tok),
        out_shape=jax.ShapeDtypeStruct((t, d), F32),
        input_output_aliases={1: 0} if layer > 0 else {},
        compiler_params=pltpu.CompilerParams(
            dimension_semantics=("parallel",), vmem_limit_bytes=VMEM_LIMIT),
        name="post_block",
    )(o, h, p, wo, g_ffn, wg, wu, wd, g_ple, wpg, wpp, g_final)


def kernel(x, p, norm_mix, norm_ffn, norm_ple, norm_final, da_w_qkv, da_w_o, da_lambda_q1, da_lambda_k1, da_lambda_q2, da_lambda_k2, da_subln, na_w_qkv, na_b_qkv, na_rpb, na_w_o, ffn_w_gate, ffn_w_up, ffn_w_down, ple_w_proj, ple_w_gate):
    b, s_len, d = x.shape
    t = b * s_len
    row = lambda a: a.reshape(1, -1).astype(F32)
    q_scale = jnp.concatenate([jnp.full((d,), DA_HEAD_DIM ** -0.5, F32), jnp.ones((2 * d,), F32)])
    q_scale = q_scale.reshape(1, -1)
    p = p.reshape(DEPTH, t, PLE_DIM)
    h = x.reshape(t, d)
    for i in range(DEPTH):
        j = i // 2
        if i % 2 == 0:
            qkv = _qkv_proj(h, row(norm_mix[i]), da_w_qkv[j].astype(BF16),
                            jnp.zeros((1, 3 * d), F32), q_scale)
            mix = _diff_attention(qkv, b, da_lambda_q1[j], da_lambda_k1[j],
                                  da_lambda_q2[j], da_lambda_k2[j], da_subln[j], i)
            w_o = da_w_o[j]
        else:
            qkv = _qkv_proj(h, row(norm_mix[i]), na_w_qkv[j].astype(BF16),
                            row(na_b_qkv[j]), q_scale)
            mix = _neighborhood_attention(qkv, b, na_rpb[j])
            w_o = na_w_o[j]
        h = _post_block(mix, h, p, i, w_o.astype(BF16), row(norm_ffn[i]),
                        ffn_w_gate[i].astype(BF16), ffn_w_up[i].astype(BF16),
                        ffn_w_down[i].astype(BF16), row(norm_ple[i]),
                        ple_w_gate[i].astype(BF16), ple_w_proj[i].astype(BF16),
                        row(norm_final), final=(i == DEPTH - 1))
    return h.reshape(b, s_len, d)
```

```python
import functools
import math

import numpy as np
import jax
import jax.numpy as jnp
from jax import lax
from jax.experimental import pallas as pl
from jax.experimental.pallas import tpu as pltpu

F32 = jnp.float32
BF16 = jnp.bfloat16

D_MODEL = 1024
DEPTH = 4
PLE_DIM = 256
FFN_HIDDEN = 2816
DA_HEADS = 8
DA_HEAD_DIM = 64
NA_HEADS = 16
NA_HEAD_DIM = 64
GRID_W = 64
NA_KR = 8
NA_KC = 16
NORM_EPS = 1e-6
SUBLN_EPS = 1e-5

LANES = 128
MXU_WIDTH = 256
VMEM_LIMIT = 56 * 1024 * 1024
N_CB = D_MODEL // LANES

QKV_TM = 1024
QKV_TN = 1024
POST_TM = 512
FFN_TF = MXU_WIDTH
DA_TQ = MXU_WIDTH
DA_EXP_BLOCKS = 4
DA_ROWS = 64
DA_VPAD = 16
POS_SHIFT = 6
assert (1 << POS_SHIFT) == GRID_W == DA_HEAD_DIM
NA_SUB_ROWS = 2
NA_WIN_ROWS = NA_KR + 2
NA_UNROLL = 4


def _rms(x, g, eps):
    return x * lax.rsqrt(jnp.mean(x * x, axis=-1, keepdims=True) + eps) * g


def _sigmoid(x):
    return 1.0 / (1.0 + jnp.exp(-x))


def _dot_nt(a, b):
    return lax.dot_general(a, b, (((1,), (1,)), ((), ())), preferred_element_type=F32)


def _qkv_kernel(x_ref, g_ref, w_ref, b_ref, cs_ref, o_ref, xn_ref):
    @pl.when(pl.program_id(1) == 0)
    def _():
        xn_ref[...] = _rms(x_ref[...], g_ref[...], NORM_EPS).astype(BF16)

    acc = jnp.dot(xn_ref[...], w_ref[...], preferred_element_type=F32)
    res = ((acc + b_ref[...]) * cs_ref[...]).astype(o_ref.dtype)
    for c in range(QKV_TN // LANES):
        o_ref[c] = res[:, c * LANES:(c + 1) * LANES]


def _qkv_proj(h, g, w, b, cs):
    t, d = h.shape
    n = w.shape[1]
    cb = QKV_TN // LANES
    return pl.pallas_call(
        _qkv_kernel,
        grid=(t // QKV_TM, n // QKV_TN),
        in_specs=[
            pl.BlockSpec((QKV_TM, d), lambda i, j: (i, 0)),
            pl.BlockSpec((1, d), lambda i, j: (0, 0)),
            pl.BlockSpec((d, QKV_TN), lambda i, j: (0, j)),
            pl.BlockSpec((1, QKV_TN), lambda i, j: (0, j)),
            pl.BlockSpec((1, QKV_TN), lambda i, j: (0, j)),
        ],
        out_specs=pl.BlockSpec((cb, QKV_TM, LANES), lambda i, j: (j, i, 0)),
        out_shape=jax.ShapeDtypeStruct((n // LANES, t, LANES), BF16),
        scratch_shapes=[pltpu.VMEM((QKV_TM, d), BF16)],
        compiler_params=pltpu.CompilerParams(
            dimension_semantics=("parallel", "arbitrary"), vmem_limit_bytes=VMEM_LIMIT),
        name="qkv_proj",
    )(h, g, w, b, cs)


def _da_key_features(s_len):
    pos = np.arange(s_len)
    f = np.zeros((s_len, LANES), np.float32)
    for base in (0, DA_HEAD_DIM):
        f[:, base + 0] = pos // GRID_W
        f[:, base + 1] = pos % GRID_W
        f[:, base + 2] = 1.0
        f[:, base + 3] = 1.0
    return jnp.asarray(f, BF16)


def _da_diag_correction():
    r = np.arange(DA_TQ)
    return jnp.asarray(-2.0 * np.maximum(r[:, None] - r[None, :], 0), F32)


def _da_kernel(lam_init, nb, slopes_ref, qc_ref, qn_ref, k_ref, v_ref, kfeat_ref, corr_ref,
               lq1_ref, lk1_ref, lq2_ref, lk2_ref, sub_ref, o_ref,
               kaug_ref, vaug_ref, qv_ref, sa_ref, sb_ref, pa_ref, pb_ref):
    h = pl.program_id(1)
    t = pl.program_id(2)
    tq = DA_TQ
    slope = slopes_ref[h]
    corr = slope * corr_ref[...]

    def scores(q, i_tile, slot, s_ref):
        lane = lax.broadcasted_iota(jnp.int32, q.shape, 1)
        fidx = jnp.bitwise_and(lane, DA_HEAD_DIM - 1)
        pos = i_tile * tq + lax.broadcasted_iota(jnp.int32, q.shape, 0)
        pos_hi = jnp.right_shift(pos, POS_SHIFT).astype(F32)
        pos_lo = jnp.bitwise_and(pos, GRID_W - 1).astype(F32)
        feat = jnp.where(fidx == 0, GRID_W * slope,
                         jnp.where(fidx == 1, slope,
                                   jnp.where(fidx == 2, -GRID_W * slope * pos_hi,
                                             jnp.where(fidx == 3, -slope * pos_lo, 0.0))))
        feat_l = feat.astype(BF16)
        feat_r = (-feat).astype(BF16)
        for m in range(2):
            is_data = (lane < DA_HEAD_DIM) if m == 0 else (lane >= DA_HEAD_DIM)
            qv_ref[slot, m, 0] = jnp.where(is_data, q, feat_l)
            qv_ref[slot, m, 1] = jnp.where(is_data, q, feat_r)

        def block(m, j):
            side = jnp.where(j > i_tile, 1, 0)
            s_ref[m, j] = _dot_nt(kaug_ref[m, j * tq:(j + 1) * tq, :], qv_ref[slot, m, side])
            if j == nb - 1:
                s_ref[m, i_tile] = s_ref[m, i_tile] + corr

        return [functools.partial(block, m, j) for m in range(2) for j in range(nb)]

    def softmax_pv(s_ref, p_ref, mxu_tasks):
        n_tasks = len(mxu_tasks)
        n_pieces = 2 * (tq // LANES) * (1 + nb // DA_EXP_BLOCKS)
        done = [0, 0]

        def piece_done():
            done[0] += 1
            while done[1] < (done[0] * n_tasks) // n_pieces:
                mxu_tasks.pop(0)()
                done[1] += 1

        outs = []
        for m in range(2):
            for c in range(tq // LANES):
                cols = slice(c * LANES, (c + 1) * LANES)
                m_el = None
                for j in range(nb):
                    for r0 in range(0, tq, DA_ROWS):
                        part = s_ref[m, j, r0:r0 + DA_ROWS, cols]
                        m_el = part if m_el is None else jnp.maximum(m_el, part)
                m_row = jnp.max(m_el, axis=0, keepdims=True)
                piece_done()
                for j in range(nb):
                    for r0 in range(0, tq, DA_ROWS):
                        part = s_ref[m, j, r0:r0 + DA_ROWS, cols]
                        p_ref[m, j * tq + r0:j * tq + r0 + DA_ROWS, cols] = jnp.exp(part - m_row).astype(BF16)
                    if (j + 1) % DA_EXP_BLOCKS == 0:
                        piece_done()
            res = jnp.dot(vaug_ref[...], p_ref[m], preferred_element_type=F32)
            outs.append(res[:LANES] / res[LANES:LANES + 1])
        assert not mxu_tasks
        lam = (jnp.exp(jnp.sum(lq1_ref[...] * lk1_ref[...], axis=-1, keepdims=True))
               - jnp.exp(jnp.sum(lq2_ref[...] * lk2_ref[...], axis=-1, keepdims=True))
               + lam_init)
        o = outs[0] - lam * outs[1]
        o = o * lax.rsqrt(jnp.mean(o * o, axis=0, keepdims=True) + SUBLN_EPS)
        o = o * sub_ref[...] * (1.0 - lam_init)
        return o.T.astype(o_ref.dtype)

    @pl.when(t == 0)
    def _():
        k = k_ref[...]
        kf = kfeat_ref[...]
        lane_k = lax.broadcasted_iota(jnp.int32, k.shape, 1)
        kaug_ref[0] = jnp.where(lane_k < DA_HEAD_DIM, k, kf)
        kaug_ref[1] = jnp.where(lane_k >= DA_HEAD_DIM, k, kf)
        for c in range(k.shape[0] // LANES):
            cols = slice(c * LANES, (c + 1) * LANES)
            vaug_ref[:LANES, cols] = v_ref[cols, :].astype(F32).T.astype(BF16)
        pad_row = lax.broadcasted_iota(jnp.int32, (DA_VPAD, k.shape[0]), 0)
        vaug_ref[LANES:, :] = jnp.where(pad_row == 0, 1.0, 0.0).astype(BF16)
        for task in scores(qc_ref[:tq, :], 2 * t, 0, sa_ref):
            task()

    tasks_b = scores(qc_ref[tq:, :], 2 * t + 1, 1, sb_ref)
    o_ref[:tq, :] = softmax_pv(sa_ref, pa_ref, tasks_b)
    tasks_n = scores(qn_ref[...], jnp.minimum(2 * t + 2, nb - 1), 2, sa_ref)
    o_ref[tq:, :] = softmax_pv(sb_ref, pb_ref, tasks_n)


def _diff_attention(qkv, batch, lq1, lk1, lq2, lk2, subln, layer_idx):
    t = qkv.shape[1]
    s_len = t // batch
    nb = s_len // DA_TQ
    lam_init = 0.8 - 0.6 * math.exp(-0.3 * layer_idx)
    slopes = [2.0 ** (-8.0 * (hh + 1) / DA_HEADS) for hh in range(DA_HEADS)]
    for sl in slopes:
        assert float(np.asarray(sl * GRID_W * (GRID_W - 1), dtype=BF16)) == sl * GRID_W * (GRID_W - 1)
    assert s_len <= GRID_W * GRID_W and nb % 2 == 0
    steps = nb // 2
    vec = lambda a: a.reshape(1, -1).astype(F32)
    small = pl.BlockSpec((1, DA_HEAD_DIM), lambda bb, hh, ii: (0, 0))
    return pl.pallas_call(
        functools.partial(_da_kernel, lam_init, nb),
        grid=(batch, DA_HEADS, steps),
        in_specs=[
            pl.BlockSpec(memory_space=pltpu.SMEM),
            pl.BlockSpec((None, 2 * DA_TQ, LANES), lambda bb, hh, ii: (hh, bb * steps + ii, 0)),
            pl.BlockSpec((None, DA_TQ, LANES),
                         lambda bb, hh, ii: (hh, bb * nb + jnp.minimum(2 * ii + 2, nb - 1), 0)),
            pl.BlockSpec((None, s_len, LANES), lambda bb, hh, ii: (DA_HEADS + hh, bb, 0)),
            pl.BlockSpec((None, s_len, LANES), lambda bb, hh, ii: (2 * DA_HEADS + hh, bb, 0)),
            pl.BlockSpec((s_len, LANES), lambda bb, hh, ii: (0, 0)),
            pl.BlockSpec((DA_TQ, DA_TQ), lambda bb, hh, ii: (0, 0)),
            small, small, small, small,
            pl.BlockSpec((LANES, 1), lambda bb, hh, ii: (0, 0)),
        ],
        out_specs=pl.BlockSpec((None, 2 * DA_TQ, LANES), lambda bb, hh, ii: (hh, bb * steps + ii, 0)),
        out_shape=jax.ShapeDtypeStruct((DA_HEADS, t, LANES), BF16),
        scratch_shapes=[
            pltpu.VMEM((2, s_len, LANES), BF16),
            pltpu.VMEM((LANES + DA_VPAD, s_len), BF16),
            pltpu.VMEM((3, 2, 2, DA_TQ, LANES), BF16),
            pltpu.VMEM((2, nb, DA_TQ, DA_TQ), F32),
            pltpu.VMEM((2, nb, DA_TQ, DA_TQ), F32),
            pltpu.VMEM((2, s_len, DA_TQ), BF16),
            pltpu.VMEM((2, s_len, DA_TQ), BF16),
        ],
        compiler_params=pltpu.CompilerParams(
            dimension_semantics=("parallel", "parallel", "arbitrary"), vmem_limit_bytes=VMEM_LIMIT),
        name="diff_attention",
    )(jnp.asarray(slopes, F32), qkv, qkv, qkv, qkv, _da_key_features(s_len), _da_diag_correction(),
      vec(lq1), vec(lk1), vec(lq2), vec(lk2), subln.reshape(-1, 1).astype(F32))


def _na_sub_geometry(rows):
    last = rows - NA_WIN_ROWS
    return ((0, 0), (NA_SUB_ROWS, 0), (2 * NA_SUB_ROWS, 0),
            (rows - 2 * NA_SUB_ROWS, last), (rows - NA_SUB_ROWS, last))


def _na_bias_tables(rpb, rows):
    w = GRID_W
    kr = min(NA_KR, rows)
    n_dr = 2 * NA_KR - 1
    cq, ck = np.arange(w)[:, None], np.arange(w)[None, :]
    cstart = np.clip(cq - NA_KC // 2, 0, w - NA_KC)
    col_ok = (ck >= cstart) & (ck < cstart + NA_KC)
    col_off = np.clip(ck - cq, -(NA_KC - 1), NA_KC - 1) + (NA_KC - 1)
    onehot = (col_off[..., None] == np.arange(2 * NA_KC - 1)).astype(np.float32)
    tiles = jnp.einsum("hdo,qko->hdqk", rpb.astype(F32), onehot, precision=lax.Precision.HIGHEST)
    tiles = jnp.where(col_ok, tiles, -jnp.inf)
    masked = jnp.full((NA_HEADS, 1, w, w), -jnp.inf, F32)
    tiles = jnp.concatenate([tiles, masked], axis=1)
    tables = []
    for rq0, ws in _na_sub_geometry(rows):
        block_rows = []
        for qi in range(NA_SUB_ROWS):
            rq = rq0 + qi
            rs = min(max(rq - kr // 2, 0), rows - kr)
            picks = []
            for kj in range(NA_WIN_ROWS):
                rk = ws + kj
                picks.append(rk - rq + (NA_KR - 1) if rs <= rk < rs + kr else n_dr)
            block_rows.append(jnp.concatenate([tiles[:, dr] for dr in picks], axis=-1))
        tables.append(jnp.concatenate(block_rows, axis=1))
    t = jnp.stack(tables, axis=1)
    nq, nk = t.shape[2], t.shape[3]
    t = t.reshape(NA_HEADS // 2, 2, 5, nq, nk).transpose(0, 2, 1, 3, 4)
    return t.reshape(NA_HEADS // 2, 5, 2 * nq, nk)


def _na_kernel(rows, q_ref, k_ref, v_ref, tab_ref, o_ref):
    n_sub = rows // NA_SUB_ROWS
    nq = NA_SUB_ROWS * GRID_W
    nk = NA_WIN_ROWS * GRID_W

    def body(g, carry):
        ws = jnp.clip(NA_SUB_ROWS * g - NA_KR // 2, 0, rows - NA_WIN_ROWS)
        kind = jnp.where(g < 2, g, jnp.where(g >= n_sub - 2, g - (n_sub - 5), 2))
        q0 = pl.multiple_of(g * nq, nq)
        k0 = pl.multiple_of(ws * GRID_W, LANES)
        qg = q_ref[pl.ds(q0, nq), :]
        lane = lax.broadcasted_iota(jnp.int32, qg.shape, 1)
        zero = jnp.zeros_like(qg)
        q2 = jnp.concatenate([jnp.where(lane < NA_HEAD_DIM, qg, zero),
                              jnp.where(lane >= NA_HEAD_DIM, qg, zero)], axis=0)
        kw = k_ref[pl.ds(k0, nk), :]
        vw = v_ref[pl.ds(k0, nk), :]
        s = _dot_nt(q2, kw) + tab_ref[0, kind]
        m = jnp.max(s, axis=-1, keepdims=True)
        p = jnp.exp(s - m)
        l = jnp.sum(p, axis=-1, keepdims=True)
        pv = jnp.dot(p.astype(BF16), vw, preferred_element_type=F32) / l
        o_ref[pl.ds(q0, nq), :] = jnp.where(lane < NA_HEAD_DIM, pv[:nq], pv[nq:]).astype(o_ref.dtype)
        return carry

    lax.fori_loop(0, n_sub, body, 0, unroll=NA_UNROLL)


def _neighborhood_attention(qkv, batch, rpb):
    t = qkv.shape[1]
    s_len = t // batch
    rows = s_len // GRID_W
    assert rows % NA_SUB_ROWS == 0 and rows >= NA_WIN_ROWS + 2 * NA_SUB_ROWS and NA_KR % 2 == 0
    tables = _na_bias_tables(rpb, rows)
    hp = NA_HEADS // 2
    slab = lambda part: pl.BlockSpec((None, s_len, LANES), lambda pp, bb: (part * hp + pp, bb, 0))
    return pl.pallas_call(
        functools.partial(_na_kernel, rows),
        grid=(hp, batch),
        in_specs=[slab(0), slab(1), slab(2),
                  pl.BlockSpec((1,) + tables.shape[1:], lambda pp, bb: (pp, 0, 0, 0))],
        out_specs=slab(0),
        out_shape=jax.ShapeDtypeStruct((hp, t, LANES), BF16),
        compiler_params=pltpu.CompilerParams(
            dimension_semantics=("parallel", "arbitrary"), vmem_limit_bytes=VMEM_LIMIT),
        name="neighborhood_attention",
    )(qkv, qkv, qkv, tables)


def _post_kernel(final, o_ref, h_ref, p_ref, wo_ref, gf_ref, wg_ref, wu_ref, wd_ref,
                 gp_ref, wpg_ref, wpp_ref, gfin_ref, out_ref):
    o = jnp.concatenate([o_ref[c] for c in range(N_CB)], axis=-1)
    h = h_ref[...] + jnp.dot(o, wo_ref[...], preferred_element_type=F32)

    n = _rms(h, gf_ref[...], NORM_EPS).astype(BF16)
    acc = h
    for c in range(FFN_HIDDEN // FFN_TF):
        cols = slice(c * FFN_TF, (c + 1) * FFN_TF)
        gate = jnp.dot(n, wg_ref[:, cols], preferred_element_type=F32)
        up = jnp.dot(n, wu_ref[:, cols], preferred_element_type=F32)
        act = (gate * _sigmoid(gate) * up).astype(BF16)
        acc = acc + jnp.dot(act, wd_ref[cols, :], preferred_element_type=F32)
    h = acc

    n = _rms(h, gp_ref[...], NORM_EPS).astype(BF16)
    gate = _sigmoid(jnp.dot(n, wpg_ref[...], preferred_element_type=F32))
    proj = jnp.dot(p_ref[...].astype(BF16), wpp_ref[...], preferred_element_type=F32)
    h = h + gate * proj
    if final:
        h = _rms(h, gfin_ref[...], NORM_EPS)
    out_ref[...] = h


def _post_block(o, h, p, layer, wo, g_ffn, wg, wu, wd, g_ple, wpg, wpp, g_final, final):
    t, d = h.shape
    tok = lambda i: (i, 0)
    const = lambda i: (0, 0)
    resident = lambda a: pl.BlockSpec(a.shape, const, pipeline_mode=pl.Buffered(1))
    return pl.pallas_call(
        functools.partial(_post_kernel, final),
        grid=(t // POST_TM,),
        in_specs=[
            pl.BlockSpec((N_CB, POST_TM, LANES), lambda i: (0, i, 0)),
            pl.BlockSpec((POST_TM, d), tok),
            pl.BlockSpec((None, POST_TM, PLE_DIM), lambda i: (layer, i, 0)),
            resident(wo), resident(g_ffn), resident(wg), resident(wu), resident(wd),
            resident(g_ple), resident(wpg), resident(wpp), resident(g_final),
        ],
        out_specs=pl.BlockSpec((POST_TM, d), tok),
        out_shape=jax.ShapeDtypeStruct((t, d), F32),
        input_output_aliases={1: 0} if layer > 0 else {},
        compiler_params=pltpu.CompilerParams(
            dimension_semantics=("parallel",), vmem_limit_bytes=VMEM_LIMIT),
        name="post_block",
    )(o, h, p, wo, g_ffn, wg, wu, wd, g_ple, wpg, wpp, g_final)


def kernel(x, p, norm_mix, norm_ffn, norm_ple, norm_final, da_w_qkv, da_w_o, da_lambda_q1, da_lambda_k1, da_lambda_q2, da_lambda_k2, da_subln, na_w_qkv, na_b_qkv, na_rpb, na_w_o, ffn_w_gate, ffn_w_up, ffn_w_down, ple_w_proj, ple_w_gate):
    b, s_len, d = x.shape
    t = b * s_len
    row = lambda a: a.reshape(1, -1).astype(F32)
    q_scale = jnp.concatenate([jnp.full((d,), DA_HEAD_DIM ** -0.5, F32), jnp.ones((2 * d,), F32)])
    q_scale = q_scale.reshape(1, -1)
    p = p.reshape(DEPTH, t, PLE_DIM)
    h = x.reshape(t, d)
    for i in range(DEPTH):
        j = i // 2
        if i % 2 == 0:
            qkv = _qkv_proj(h, row(norm_mix[i]), da_w_qkv[j].astype(BF16),
                            jnp.zeros((1, 3 * d), F32), q_scale)
            mix = _diff_attention(qkv, b, da_lambda_q1[j], da_lambda_k1[j],
                                  da_lambda_q2[j], da_lambda_k2[j], da_subln[j], i)
            w_o = da_w_o[j]
        else:
            qkv = _qkv_proj(h, row(norm_mix[i]), na_w_qkv[j].astype(BF16),
                            row(na_b_qkv[j]), q_scale)
            mix = _neighborhood_attention(qkv, b, na_rpb[j])
            w_o = na_w_o[j]
        h = _post_block(mix, h, p, i, w_o.astype(BF16), row(norm_ffn[i]),
                        ffn_w_gate[i].astype(BF16), ffn_w_up[i].astype(BF16),
                        ffn_w_down[i].astype(BF16), row(norm_ple[i]),
                        ple_w_gate[i].astype(BF16), ple_w_proj[i].astype(BF16),
                        row(norm_final), final=(i == DEPTH - 1))
    return h.reshape(b, s_len, d)
```

```python
import functools
import math

import numpy as np
import jax
import jax.numpy as jnp
from jax import lax
from jax.experimental import pallas as pl
from jax.experimental.pallas import tpu as pltpu

F32 = jnp.float32
BF16 = jnp.bfloat16

D_MODEL = 1024
DEPTH = 4
PLE_DIM = 256
FFN_HIDDEN = 2816
DA_HEADS = 8
DA_HEAD_DIM = 64
NA_HEADS = 16
NA_HEAD_DIM = 64
GRID_W = 64
NA_KR = 8
NA_KC = 16
NORM_EPS = 1e-6
SUBLN_EPS = 1e-5

LANES = 128
MXU_WIDTH = 256
VMEM_LIMIT = 56 * 1024 * 1024
N_CB = D_MODEL // LANES

QKV_TM = 1024
QKV_TN = 1024
POST_TM = 512
FFN_TF = MXU_WIDTH
DA_TQ = MXU_WIDTH
DA_RC = 32
POS_SHIFT = 6
assert (1 << POS_SHIFT) == GRID_W == DA_HEAD_DIM
NA_SUB_ROWS = 2
NA_WIN_ROWS = NA_KR + 2
NA_STAGES = 3


def _rms(x, g, eps):
    return x * lax.rsqrt(jnp.mean(x * x, axis=-1, keepdims=True) + eps) * g


def _sigmoid(x):
    return 1.0 / (1.0 + jnp.exp(-x))


def _dot_nt(a, b):
    return lax.dot_general(a, b, (((1,), (1,)), ((), ())), preferred_element_type=F32)


def _qkv_kernel(x_ref, g_ref, w_ref, b_ref, cs_ref, o_ref, xn_ref):
    @pl.when(pl.program_id(1) == 0)
    def _():
        xn_ref[...] = _rms(x_ref[...], g_ref[...], NORM_EPS).astype(BF16)

    acc = jnp.dot(xn_ref[...], w_ref[...], preferred_element_type=F32)
    res = ((acc + b_ref[...]) * cs_ref[...]).astype(o_ref.dtype)
    for c in range(QKV_TN // LANES):
        o_ref[c] = res[:, c * LANES:(c + 1) * LANES]


def _qkv_proj(h, g, w, b, cs):
    t, d = h.shape
    n = w.shape[1]
    cb = QKV_TN // LANES
    return pl.pallas_call(
        _qkv_kernel,
        grid=(t // QKV_TM, n // QKV_TN),
        in_specs=[
            pl.BlockSpec((QKV_TM, d), lambda i, j: (i, 0)),
            pl.BlockSpec((1, d), lambda i, j: (0, 0)),
            pl.BlockSpec((d, QKV_TN), lambda i, j: (0, j)),
            pl.BlockSpec((1, QKV_TN), lambda i, j: (0, j)),
            pl.BlockSpec((1, QKV_TN), lambda i, j: (0, j)),
        ],
        out_specs=pl.BlockSpec((cb, QKV_TM, LANES), lambda i, j: (j, i, 0)),
        out_shape=jax.ShapeDtypeStruct((n // LANES, t, LANES), BF16),
        scratch_shapes=[pltpu.VMEM((QKV_TM, d), BF16)],
        compiler_params=pltpu.CompilerParams(
            dimension_semantics=("parallel", "arbitrary"), vmem_limit_bytes=VMEM_LIMIT),
        name="qkv_proj",
    )(h, g, w, b, cs)


def _da_key_features(s_len):
    pos = np.arange(s_len)
    f = np.zeros((s_len, LANES), np.float32)
    for base in (0, DA_HEAD_DIM):
        f[:, base + 0] = pos // GRID_W
        f[:, base + 1] = pos % GRID_W
        f[:, base + 2] = 1.0
        f[:, base + 3] = 1.0
    return jnp.asarray(f, BF16)


def _da_diag_correction():
    r = np.arange(DA_TQ)
    return jnp.asarray(-2.0 * np.maximum(r[None, :] - r[:, None], 0), F32)


def _da_kernel(lam_init, nb, slopes_ref, qc_ref, qn_ref, k_ref, v_ref, kfeat_ref, corr_ref,
               lq1_ref, lk1_ref, lq2_ref, lk2_ref, sub_ref, o_ref,
               kaug_ref, vaug_ref, qv_ref, sa_ref, sb_ref, pa_ref, pb_ref):
    h = pl.program_id(1)
    t = pl.program_id(2)
    tq = DA_TQ
    slope = slopes_ref[h]
    corr = slope * corr_ref[...]

    def scores(q, i_tile, slot, s_ref):
        lane = lax.broadcasted_iota(jnp.int32, q.shape, 1)
        fidx = jnp.bitwise_and(lane, DA_HEAD_DIM - 1)
        pos = i_tile * tq + lax.broadcasted_iota(jnp.int32, q.shape, 0)
        pos_hi = jnp.right_shift(pos, POS_SHIFT).astype(F32)
        pos_lo = jnp.bitwise_and(pos, GRID_W - 1).astype(F32)
        feat = jnp.where(fidx == 0, GRID_W * slope,
                         jnp.where(fidx == 1, slope,
                                   jnp.where(fidx == 2, -GRID_W * slope * pos_hi,
                                             jnp.where(fidx == 3, -slope * pos_lo, 0.0))))
        feat_l = feat.astype(BF16)
        feat_r = (-feat).astype(BF16)
        for m in range(2):
            is_data = (lane < DA_HEAD_DIM) if m == 0 else (lane >= DA_HEAD_DIM)
            qv_ref[slot, m, 0] = jnp.where(is_data, q, feat_l)
            qv_ref[slot, m, 1] = jnp.where(is_data, q, feat_r)

        def block(m, j):
            side = jnp.where(j > i_tile, 1, 0)
            s_ref[m, j] = _dot_nt(qv_ref[slot, m, side], kaug_ref[m, j * tq:(j + 1) * tq, :])
            if j == nb - 1:
                s_ref[m, i_tile] = s_ref[m, i_tile] + corr

        return [functools.partial(block, m, j) for m in range(2) for j in range(nb)]

    def softmax_chunk(s_ref, p_ref, m, rc):
        rows = slice(rc * DA_RC, (rc + 1) * DA_RC)
        m_el = s_ref[m, 0, rows, :]
        for j in range(1, nb):
            m_el = jnp.maximum(m_el, s_ref[m, j, rows, :])
        mb = jnp.broadcast_to(jnp.max(m_el, axis=-1, keepdims=True), m_el.shape)
        for j in range(nb):
            p_ref[m, rows, j * tq:(j + 1) * tq] = jnp.exp(s_ref[m, j, rows, :] - mb).astype(BF16)

    def softmax_pv(s_ref, p_ref, mxu_tasks):
        n_chunks = tq // DA_RC
        per_chunk = len(mxu_tasks) // (2 * n_chunks)
        outs = []
        for m in range(2):
            for rc in range(n_chunks):
                softmax_chunk(s_ref, p_ref, m, rc)
                for _ in range(per_chunk):
                    mxu_tasks.pop(0)()
            res = jnp.dot(p_ref[m], vaug_ref[...], preferred_element_type=F32)
            outs.append(res[:, :LANES] / res[:, LANES:])
        while mxu_tasks:
            mxu_tasks.pop(0)()
        lam = (jnp.exp(jnp.sum(lq1_ref[...] * lk1_ref[...], axis=-1, keepdims=True))
               - jnp.exp(jnp.sum(lq2_ref[...] * lk2_ref[...], axis=-1, keepdims=True))
               + lam_init)
        o = outs[0] - lam * outs[1]
        o = _rms(o, sub_ref[...], SUBLN_EPS) * (1.0 - lam_init)
        return o.astype(o_ref.dtype)

    @pl.when(t == 0)
    def _():
        k = k_ref[...]
        kf = kfeat_ref[...]
        lane_k = lax.broadcasted_iota(jnp.int32, k.shape, 1)
        kaug_ref[0] = jnp.where(lane_k < DA_HEAD_DIM, k, kf)
        kaug_ref[1] = jnp.where(lane_k >= DA_HEAD_DIM, k, kf)
        vaug_ref[:, :LANES] = v_ref[...]
        vaug_ref[:, LANES:] = jnp.ones(v_ref.shape, BF16)
        for task in scores(qc_ref[:tq, :], 2 * t, 0, sa_ref):
            task()

    tasks_b = scores(qc_ref[tq:, :], 2 * t + 1, 1, sb_ref)
    o_ref[:tq, :] = softmax_pv(sa_ref, pa_ref, tasks_b)
    tasks_n = scores(qn_ref[...], jnp.minimum(2 * t + 2, nb - 1), 2, sa_ref)
    o_ref[tq:, :] = softmax_pv(sb_ref, pb_ref, tasks_n)


def _diff_attention(qkv, batch, lq1, lk1, lq2, lk2, subln, layer_idx):
    t = qkv.shape[1]
    s_len = t // batch
    nb = s_len // DA_TQ
    lam_init = 0.8 - 0.6 * math.exp(-0.3 * layer_idx)
    slopes = [2.0 ** (-8.0 * (hh + 1) / DA_HEADS) for hh in range(DA_HEADS)]
    for sl in slopes:
        assert float(np.asarray(sl * GRID_W * (GRID_W - 1), dtype=BF16)) == sl * GRID_W * (GRID_W - 1)
    assert s_len <= GRID_W * GRID_W and nb % 2 == 0
    steps = nb // 2
    vec = lambda a: a.reshape(1, -1).astype(F32)
    small = pl.BlockSpec((1, DA_HEAD_DIM), lambda bb, hh, ii: (0, 0))
    return pl.pallas_call(
        functools.partial(_da_kernel, lam_init, nb),
        grid=(batch, DA_HEADS, steps),
        in_specs=[
            pl.BlockSpec(memory_space=pltpu.SMEM),
            pl.BlockSpec((None, 2 * DA_TQ, LANES), lambda bb, hh, ii: (hh, bb * steps + ii, 0)),
            pl.BlockSpec((None, DA_TQ, LANES),
                         lambda bb, hh, ii: (hh, bb * nb + jnp.minimum(2 * ii + 2, nb - 1), 0)),
            pl.BlockSpec((None, s_len, LANES), lambda bb, hh, ii: (DA_HEADS + hh, bb, 0)),
            pl.BlockSpec((None, s_len, LANES), lambda bb, hh, ii: (2 * DA_HEADS + hh, bb, 0)),
            pl.BlockSpec((s_len, LANES), lambda bb, hh, ii: (0, 0)),
            pl.BlockSpec((DA_TQ, DA_TQ), lambda bb, hh, ii: (0, 0)),
            small, small, small, small,
            pl.BlockSpec((1, LANES), lambda bb, hh, ii: (0, 0)),
        ],
        out_specs=pl.BlockSpec((None, 2 * DA_TQ, LANES), lambda bb, hh, ii: (hh, bb * steps + ii, 0)),
        out_shape=jax.ShapeDtypeStruct((DA_HEADS, t, LANES), BF16),
        scratch_shapes=[
            pltpu.VMEM((2, s_len, LANES), BF16),
            pltpu.VMEM((s_len, 2 * LANES), BF16),
            pltpu.VMEM((3, 2, 2, DA_TQ, LANES), BF16),
            pltpu.VMEM((2, nb, DA_TQ, DA_TQ), F32),
            pltpu.VMEM((2, nb, DA_TQ, DA_TQ), F32),
            pltpu.VMEM((2, DA_TQ, s_len), BF16),
            pltpu.VMEM((2, DA_TQ, s_len), BF16),
        ],
        compiler_params=pltpu.CompilerParams(
            dimension_semantics=("parallel", "parallel", "arbitrary"), vmem_limit_bytes=VMEM_LIMIT),
        name="diff_attention",
    )(jnp.asarray(slopes, F32), qkv, qkv, qkv, qkv, _da_key_features(s_len), _da_diag_correction(),
      vec(lq1), vec(lk1), vec(lq2), vec(lk2), vec(subln))


def _na_sub_geometry(rows):
    last = rows - NA_WIN_ROWS
    return ((0, 0), (NA_SUB_ROWS, 0), (2 * NA_SUB_ROWS, 0),
            (rows - 2 * NA_SUB_ROWS, last), (rows - NA_SUB_ROWS, last))


def _na_bias_tables(rpb, rows):
    w = GRID_W
    kr = min(NA_KR, rows)
    n_dr = 2 * NA_KR - 1
    cq, ck = np.arange(w)[:, None], np.arange(w)[None, :]
    cstart = np.clip(cq - NA_KC // 2, 0, w - NA_KC)
    col_ok = (ck >= cstart) & (ck < cstart + NA_KC)
    col_off = np.clip(ck - cq, -(NA_KC - 1), NA_KC - 1) + (NA_KC - 1)
    onehot = (col_off[..., None] == np.arange(2 * NA_KC - 1)).astype(np.float32)
    tiles = jnp.einsum("hdo,qko->hdqk", rpb.astype(F32), onehot, precision=lax.Precision.HIGHEST)
    tiles = jnp.where(col_ok, tiles, -jnp.inf)
    masked = jnp.full((NA_HEADS, 1, w, w), -jnp.inf, F32)
    tiles = jnp.concatenate([tiles, masked], axis=1)
    tables = []
    for rq0, ws in _na_sub_geometry(rows):
        block_rows = []
        for qi in range(NA_SUB_ROWS):
            rq = rq0 + qi
            rs = min(max(rq - kr // 2, 0), rows - kr)
            picks = []
            for kj in range(NA_WIN_ROWS):
                rk = ws + kj
                picks.append(rk - rq + (NA_KR - 1) if rs <= rk < rs + kr else n_dr)
            block_rows.append(jnp.concatenate([tiles[:, dr] for dr in picks], axis=-1))
        tables.append(jnp.concatenate(block_rows, axis=1))
    t = jnp.stack(tables, axis=1)
    nq, nk = t.shape[2], t.shape[3]
    t = t.reshape(NA_HEADS // 2, 2, 5, nq, nk).transpose(0, 2, 1, 3, 4)
    return t.reshape(NA_HEADS // 2, 5, 2 * nq, nk)


def _na_kernel(rows, q_ref, k_ref, v_ref, tab_ref, o_ref, s_ref, p_ref, l_ref):
    n_sub = rows // NA_SUB_ROWS
    nq = NA_SUB_ROWS * GRID_W
    nk = NA_WIN_ROWS * GRID_W
    lane = lax.broadcasted_iota(jnp.int32, (nq, LANES), 1)

    def geometry(g):
        ws = min(max(NA_SUB_ROWS * g - NA_KR // 2, 0), rows - NA_WIN_ROWS)
        kind = g if g < 2 else (g - (n_sub - 5) if g >= n_sub - 2 else 2)
        return g * nq, ws * GRID_W, kind

    def stage_scores(g):
        q0, k0, kind = geometry(g)
        qg = q_ref[q0:q0 + nq, :]
        zero = jnp.zeros_like(qg)
        q2 = jnp.concatenate([jnp.where(lane < NA_HEAD_DIM, qg, zero),
                              jnp.where(lane >= NA_HEAD_DIM, qg, zero)], axis=0)
        s_ref[g % NA_STAGES] = _dot_nt(q2, k_ref[k0:k0 + nk, :]) + tab_ref[0, kind]

    def stage_softmax(g):
        s = s_ref[g % NA_STAGES]
        p = jnp.exp(s - jnp.max(s, axis=-1, keepdims=True))
        l_ref[g % NA_STAGES] = jnp.sum(p, axis=-1, keepdims=True)
        p_ref[g % NA_STAGES] = p.astype(BF16)

    def stage_pv(g):
        q0, k0, _ = geometry(g)
        pv = jnp.dot(p_ref[g % NA_STAGES], v_ref[k0:k0 + nk, :], preferred_element_type=F32)
        pv = pv / l_ref[g % NA_STAGES]
        o_ref[q0:q0 + nq, :] = jnp.where(lane < NA_HEAD_DIM, pv[:nq], pv[nq:]).astype(o_ref.dtype)

    for step in range(n_sub + 2):
        if step < n_sub:
            stage_scores(step)
        if 0 <= step - 1 < n_sub:
            stage_softmax(step - 1)
        if 0 <= step - 2 < n_sub:
            stage_pv(step - 2)


def _neighborhood_attention(qkv, batch, rpb):
    t = qkv.shape[1]
    s_len = t // batch
    rows = s_len // GRID_W
    assert rows % NA_SUB_ROWS == 0 and rows >= NA_WIN_ROWS + 2 * NA_SUB_ROWS and NA_KR % 2 == 0
    tables = _na_bias_tables(rpb, rows)
    hp = NA_HEADS // 2
    slab = lambda part: pl.BlockSpec((None, s_len, LANES), lambda pp, bb: (part * hp + pp, bb, 0))
    return pl.pallas_call(
        functools.partial(_na_kernel, rows),
        grid=(hp, batch),
        in_specs=[slab(0), slab(1), slab(2),
                  pl.BlockSpec((1,) + tables.shape[1:], lambda pp, bb: (pp, 0, 0, 0))],
        out_specs=slab(0),
        out_shape=jax.ShapeDtypeStruct((hp, t, LANES), BF16),
        scratch_shapes=[
            pltpu.VMEM((NA_STAGES,) + tables.shape[2:], F32),
            pltpu.VMEM((NA_STAGES,) + tables.shape[2:], BF16),
            pltpu.VMEM((NA_STAGES, tables.shape[2], 1), F32),
        ],
        compiler_params=pltpu.CompilerParams(
            dimension_semantics=("parallel", "arbitrary"), vmem_limit_bytes=VMEM_LIMIT),
        name="neighborhood_attention",
    )(qkv, qkv, qkv, tables)


def _post_kernel(final, o_ref, h_ref, p_ref, wo_ref, gf_ref, wg_ref, wu_ref, wd_ref,
                 gp_ref, wpg_ref, wpp_ref, gfin_ref, out_ref):
    o = jnp.concatenate([o_ref[c] for c in range(N_CB)], axis=-1)
    h = h_ref[...] + jnp.dot(o, wo_ref[...], preferred_element_type=F32)

    n = _rms(h, gf_ref[...], NORM_EPS).astype(BF16)
    acc = h
    for c in range(FFN_HIDDEN // FFN_TF):
        cols = slice(c * FFN_TF, (c + 1) * FFN_TF)
        gate = jnp.dot(n, wg_ref[:, cols], preferred_element_type=F32)
        up = jnp.dot(n, wu_ref[:, cols], preferred_element_type=F32)
        act = (gate * _sigmoid(gate) * up).astype(BF16)
        acc = acc + jnp.dot(act, wd_ref[cols, :], preferred_element_type=F32)
    h = acc

    n = _rms(h, gp_ref[...], NORM_EPS).astype(BF16)
    gate = _sigmoid(jnp.dot(n, wpg_ref[...], preferred_element_type=F32))
    proj = jnp.dot(p_ref[...].astype(BF16), wpp_ref[...], preferred_element_type=F32)
    h = h + gate * proj
    if final:
        h = _rms(h, gfin_ref[...], NORM_EPS)
    out_ref[...] = h


def _post_block(o, h, p, layer, wo, g_ffn, wg, wu, wd, g_ple, wpg, wpp, g_final, final):
    t, d = h.shape
    tok = lambda i: (i, 0)
    const = lambda i: (0, 0)
    resident = lambda a: pl.BlockSpec(a.shape, const, pipeline_mode=pl.Buffered(1))
    return pl.pallas_call(
        functools.partial(_post_kernel, final),
        grid=(t // POST_TM,),
        in_specs=[
            pl.BlockSpec((N_CB, POST_TM, LANES), lambda i: (0, i, 0)),
            pl.BlockSpec((POST_TM, d), tok),
            pl.BlockSpec((None, POST_TM, PLE_DIM), lambda i: (layer, i, 0)),
            resident(wo), resident(g_ffn), resident(wg), resident(wu), resident(wd),
            resident(g_ple), resident(wpg), resident(wpp), resident(g_final),
        ],
        out_specs=pl.BlockSpec((POST_TM, d), tok),
        out_shape=jax.ShapeDtypeStruct((t, d), F32),
        input_output_aliases={1: 0} if layer > 0 else {},
        compiler_params=pltpu.CompilerParams(
            dimension_semantics=("parallel",), vmem_limit_bytes=VMEM_LIMIT),
        name="post_block",
    )(o, h, p, wo, g_ffn, wg, wu, wd, g_ple, wpg, wpp, g_final)


def kernel(x, p, norm_mix, norm_ffn, norm_ple, norm_final, da_w_qkv, da_w_o, da_lambda_q1, da_lambda_k1, da_lambda_q2, da_lambda_k2, da_subln, na_w_qkv, na_b_qkv, na_rpb, na_w_o, ffn_w_gate, ffn_w_up, ffn_w_down, ple_w_proj, ple_w_gate):
    b, s_len, d = x.shape
    t = b * s_len
    row = lambda a: a.reshape(1, -1).astype(F32)
    q_scale = jnp.concatenate([jnp.full((d,), DA_HEAD_DIM ** -0.5, F32), jnp.ones((2 * d,), F32)])
    q_scale = q_scale.reshape(1, -1)
    p = p.reshape(DEPTH, t, PLE_DIM)
    h = x.reshape(t, d)
    for i in range(DEPTH):
        j = i // 2
        if i % 2 == 0:
            qkv = _qkv_proj(h, row(norm_mix[i]), da_w_qkv[j].astype(BF16),
                            jnp.zeros((1, 3 * d), F32), q_scale)
            mix = _diff_attention(qkv, b, da_lambda_q1[j], da_lambda_k1[j],
                                  da_lambda_q2[j], da_lambda_k2[j], da_subln[j], i)
            w_o = da_w_o[j]
        else:
            qkv = _qkv_proj(h, row(norm_mix[i]), na_w_qkv[j].astype(BF16),
                            row(na_b_qkv[j]), q_scale)
            mix = _neighborhood_attention(qkv, b, na_rpb[j])
            w_o = na_w_o[j]
        h = _post_block(mix, h, p, i, w_o.astype(BF16), row(norm_ffn[i]),
                        ffn_w_gate[i].astype(BF16), ffn_w_up[i].astype(BF16),
                        ffn_w_down[i].astype(BF16), row(norm_ple[i]),
                        ple_w_gate[i].astype(BF16), ple_w_proj[i].astype(BF16),
                        row(norm_final), final=(i == DEPTH - 1))
    return h.reshape(b, s_len, d)
```

```python
import functools
import math

import numpy as np
import jax
import jax.numpy as jnp
from jax import lax
from jax.experimental import pallas as pl
from jax.experimental.pallas import tpu as pltpu

F32 = jnp.float32
BF16 = jnp.bfloat16

D_MODEL = 1024
DEPTH = 4
PLE_DIM = 256
FFN_HIDDEN = 2816
DA_HEADS = 8
DA_HEAD_DIM = 64
NA_HEADS = 16
NA_HEAD_DIM = 64
GRID_W = 64
NA_KR = 8
NA_KC = 16
NORM_EPS = 1e-6
SUBLN_EPS = 1e-5

LANES = 128
MXU_WIDTH = 256
VMEM_LIMIT = 56 * 1024 * 1024
N_CB = D_MODEL // LANES

QKV_TM = 1024
QKV_TN = 1024
POST_TM = 512
FFN_TF = MXU_WIDTH
DA_TQ = MXU_WIDTH
DA_RC = 32
DA_PV_SPLIT = 8
DA_UNITS_PER_CHUNK = (3, 4)
POS_SHIFT = 6
assert (1 << POS_SHIFT) == GRID_W == DA_HEAD_DIM
NA_SUB_ROWS = 2
NA_WIN_ROWS = NA_KR + 2
NA_STAGES = 3


def _rms(x, g, eps):
    return x * lax.rsqrt(jnp.mean(x * x, axis=-1, keepdims=True) + eps) * g


def _sigmoid(x):
    return 1.0 / (1.0 + jnp.exp(-x))


def _dot_nt(a, b):
    return lax.dot_general(a, b, (((1,), (1,)), ((), ())), preferred_element_type=F32)


def _qkv_kernel(x_ref, g_ref, w_ref, b_ref, cs_ref, o_ref, xn_ref):
    @pl.when(pl.program_id(1) == 0)
    def _():
        xn_ref[...] = _rms(x_ref[...], g_ref[...], NORM_EPS).astype(BF16)

    acc = jnp.dot(xn_ref[...], w_ref[...], preferred_element_type=F32)
    res = ((acc + b_ref[...]) * cs_ref[...]).astype(o_ref.dtype)
    for c in range(QKV_TN // LANES):
        o_ref[c] = res[:, c * LANES:(c + 1) * LANES]


def _qkv_proj(h, g, w, b, cs):
    t, d = h.shape
    n = w.shape[1]
    cb = QKV_TN // LANES
    return pl.pallas_call(
        _qkv_kernel,
        grid=(t // QKV_TM, n // QKV_TN),
        in_specs=[
            pl.BlockSpec((QKV_TM, d), lambda i, j: (i, 0)),
            pl.BlockSpec((1, d), lambda i, j: (0, 0)),
            pl.BlockSpec((d, QKV_TN), lambda i, j: (0, j)),
            pl.BlockSpec((1, QKV_TN), lambda i, j: (0, j)),
            pl.BlockSpec((1, QKV_TN), lambda i, j: (0, j)),
        ],
        out_specs=pl.BlockSpec((cb, QKV_TM, LANES), lambda i, j: (j, i, 0)),
        out_shape=jax.ShapeDtypeStruct((n // LANES, t, LANES), BF16),
        scratch_shapes=[pltpu.VMEM((QKV_TM, d), BF16)],
        compiler_params=pltpu.CompilerParams(
            dimension_semantics=("parallel", "arbitrary"), vmem_limit_bytes=VMEM_LIMIT),
        name="qkv_proj",
    )(h, g, w, b, cs)


def _da_key_features(s_len):
    pos = np.arange(s_len)
    f = np.zeros((s_len, LANES), np.float32)
    for base in (0, DA_HEAD_DIM):
        f[:, base + 0] = pos // GRID_W
        f[:, base + 1] = pos % GRID_W
        f[:, base + 2] = 1.0
        f[:, base + 3] = 1.0
    return jnp.asarray(f, BF16)


def _da_diag_correction():
    r = np.arange(DA_TQ)
    return jnp.asarray(-2.0 * np.maximum(r[None, :] - r[:, None], 0), F32)


def _da_kernel(lam_init, nb, slopes_ref, qc_ref, qn_ref, k_ref, v_ref, kfeat_ref, corr_ref,
               lq1_ref, lk1_ref, lq2_ref, lk2_ref, sub_ref, o_ref,
               kaug_ref, vaug_ref, qv_ref, sa_ref, sb_ref, pa_ref, pb_ref, acc_ref):
    h = pl.program_id(1)
    t = pl.program_id(2)
    tq = DA_TQ
    slope = slopes_ref[h]
    corr = slope * corr_ref[...]

    def scores(q, i_tile, slot, s_ref):
        lane = lax.broadcasted_iota(jnp.int32, q.shape, 1)
        fidx = jnp.bitwise_and(lane, DA_HEAD_DIM - 1)
        pos = i_tile * tq + lax.broadcasted_iota(jnp.int32, q.shape, 0)
        pos_hi = jnp.right_shift(pos, POS_SHIFT).astype(F32)
        pos_lo = jnp.bitwise_and(pos, GRID_W - 1).astype(F32)
        feat = jnp.where(fidx == 0, GRID_W * slope,
                         jnp.where(fidx == 1, slope,
                                   jnp.where(fidx == 2, -GRID_W * slope * pos_hi,
                                             jnp.where(fidx == 3, -slope * pos_lo, 0.0))))
        feat_l = feat.astype(BF16)
        feat_r = (-feat).astype(BF16)
        for m in range(2):
            is_data = (lane < DA_HEAD_DIM) if m == 0 else (lane >= DA_HEAD_DIM)
            qv_ref[slot, m, 0] = jnp.where(is_data, q, feat_l)
            qv_ref[slot, m, 1] = jnp.where(is_data, q, feat_r)

        def block(m, j):
            side = jnp.where(j > i_tile, 1, 0)
            s_ref[m, j] = _dot_nt(qv_ref[slot, m, side], kaug_ref[m, j * tq:(j + 1) * tq, :])
            if j == nb - 1:
                s_ref[m, i_tile] = s_ref[m, i_tile] + corr

        return [(1, functools.partial(block, m, j)) for m in range(2) for j in range(nb)]

    def pv_tasks(p_ref, tile, m):
        kc = p_ref.shape[2] // DA_PV_SPLIT

        def piece(c):
            ks = slice(c * kc, (c + 1) * kc)
            part = jnp.dot(p_ref[m, :, ks], vaug_ref[ks, :], preferred_element_type=F32)
            acc_ref[tile, m] = part if c == 0 else acc_ref[tile, m] + part

        return [(nb // DA_PV_SPLIT, functools.partial(piece, c)) for c in range(DA_PV_SPLIT)]

    queue = []
    budget = [0]

    def issue(units):
        budget[0] += units
        while queue and queue[0][0] <= budget[0]:
            cost, task = queue.pop(0)
            budget[0] -= cost
            task()

    def softmax_phase(s_ref, p_ref, tile, units_per_chunk):
        for m in range(2):
            for rc in range(tq // DA_RC):
                rows = slice(rc * DA_RC, (rc + 1) * DA_RC)
                m_el = s_ref[m, 0, rows, :]
                for j in range(1, nb):
                    m_el = jnp.maximum(m_el, s_ref[m, j, rows, :])
                mb = jnp.broadcast_to(jnp.max(m_el, axis=-1, keepdims=True), m_el.shape)
                for j in range(nb):
                    p_ref[m, rows, j * tq:(j + 1) * tq] = jnp.exp(s_ref[m, j, rows, :] - mb).astype(BF16)
                issue(units_per_chunk)
            queue.extend(pv_tasks(p_ref, tile, m))

    def finalize(tile):
        outs = [acc_ref[tile, m, :, :LANES] / acc_ref[tile, m, :, LANES:] for m in range(2)]
        lam = (jnp.exp(jnp.sum(lq1_ref[...] * lk1_ref[...], axis=-1, keepdims=True))
               - jnp.exp(jnp.sum(lq2_ref[...] * lk2_ref[...], axis=-1, keepdims=True))
               + lam_init)
        o = outs[0] - lam * outs[1]
        o = _rms(o, sub_ref[...], SUBLN_EPS) * (1.0 - lam_init)
        return o.astype(o_ref.dtype)

    @pl.when(t == 0)
    def _():
        k = k_ref[...]
        kf = kfeat_ref[...]
        lane_k = lax.broadcasted_iota(jnp.int32, k.shape, 1)
        kaug_ref[0] = jnp.where(lane_k < DA_HEAD_DIM, k, kf)
        kaug_ref[1] = jnp.where(lane_k >= DA_HEAD_DIM, k, kf)
        vaug_ref[:, :LANES] = v_ref[...]
        vaug_ref[:, LANES:] = jnp.ones(v_ref.shape, BF16)
        for _, task in scores(qc_ref[:tq, :], 2 * t, 0, sa_ref):
            task()

    queue.extend(scores(qc_ref[tq:, :], 2 * t + 1, 1, sb_ref))
    softmax_phase(sa_ref, pa_ref, 0, DA_UNITS_PER_CHUNK[0])
    queue.extend(scores(qn_ref[...], jnp.minimum(2 * t + 2, nb - 1), 2, sa_ref))
    softmax_phase(sb_ref, pb_ref, 1, DA_UNITS_PER_CHUNK[1])
    issue(sum(cost for cost, _ in queue))
    assert not queue
    o_ref[:tq, :] = finalize(0)
    o_ref[tq:, :] = finalize(1)


def _diff_attention(qkv, batch, lq1, lk1, lq2, lk2, subln, layer_idx):
    t = qkv.shape[1]
    s_len = t // batch
    nb = s_len // DA_TQ
    lam_init = 0.8 - 0.6 * math.exp(-0.3 * layer_idx)
    slopes = [2.0 ** (-8.0 * (hh + 1) / DA_HEADS) for hh in range(DA_HEADS)]
    for sl in slopes:
        assert float(np.asarray(sl * GRID_W * (GRID_W - 1), dtype=BF16)) == sl * GRID_W * (GRID_W - 1)
    assert s_len <= GRID_W * GRID_W and nb % 2 == 0
    steps = nb // 2
    vec = lambda a: a.reshape(1, -1).astype(F32)
    small = pl.BlockSpec((1, DA_HEAD_DIM), lambda bb, hh, ii: (0, 0))
    return pl.pallas_call(
        functools.partial(_da_kernel, lam_init, nb),
        grid=(batch, DA_HEADS, steps),
        in_specs=[
            pl.BlockSpec(memory_space=pltpu.SMEM),
            pl.BlockSpec((None, 2 * DA_TQ, LANES), lambda bb, hh, ii: (hh, bb * steps + ii, 0)),
            pl.BlockSpec((None, DA_TQ, LANES),
                         lambda bb, hh, ii: (hh, bb * nb + jnp.minimum(2 * ii + 2, nb - 1), 0)),
            pl.BlockSpec((None, s_len, LANES), lambda bb, hh, ii: (DA_HEADS + hh, bb, 0)),
            pl.BlockSpec((None, s_len, LANES), lambda bb, hh, ii: (2 * DA_HEADS + hh, bb, 0)),
            pl.BlockSpec((s_len, LANES), lambda bb, hh, ii: (0, 0)),
            pl.BlockSpec((DA_TQ, DA_TQ), lambda bb, hh, ii: (0, 0)),
            small, small, small, small,
            pl.BlockSpec((1, LANES), lambda bb, hh, ii: (0, 0)),
        ],
        out_specs=pl.BlockSpec((None, 2 * DA_TQ, LANES), lambda bb, hh, ii: (hh, bb * steps + ii, 0)),
        out_shape=jax.ShapeDtypeStruct((DA_HEADS, t, LANES), BF16),
        scratch_shapes=[
            pltpu.VMEM((2, s_len, LANES), BF16),
            pltpu.VMEM((s_len, 2 * LANES), BF16),
            pltpu.VMEM((3, 2, 2, DA_TQ, LANES), BF16),
            pltpu.VMEM((2, nb, DA_TQ, DA_TQ), F32),
            pltpu.VMEM((2, nb, DA_TQ, DA_TQ), F32),
            pltpu.VMEM((2, DA_TQ, s_len), BF16),
            pltpu.VMEM((2, DA_TQ, s_len), BF16),
            pltpu.VMEM((2, 2, DA_TQ, 2 * LANES), F32),
        ],
        compiler_params=pltpu.CompilerParams(
            dimension_semantics=("parallel", "parallel", "arbitrary"), vmem_limit_bytes=VMEM_LIMIT),
        name="diff_attention",
    )(jnp.asarray(slopes, F32), qkv, qkv, qkv, qkv, _da_key_features(s_len), _da_diag_correction(),
      vec(lq1), vec(lk1), vec(lq2), vec(lk2), vec(subln))


def _na_sub_geometry(rows):
    last = rows - NA_WIN_ROWS
    return ((0, 0), (NA_SUB_ROWS, 0), (2 * NA_SUB_ROWS, 0),
            (rows - 2 * NA_SUB_ROWS, last), (rows - NA_SUB_ROWS, last))


def _na_bias_tables(rpb, rows):
    w = GRID_W
    kr = min(NA_KR, rows)
    n_dr = 2 * NA_KR - 1
    cq, ck = np.arange(w)[:, None], np.arange(w)[None, :]
    cstart = np.clip(cq - NA_KC // 2, 0, w - NA_KC)
    col_ok = (ck >= cstart) & (ck < cstart + NA_KC)
    col_off = np.clip(ck - cq, -(NA_KC - 1), NA_KC - 1) + (NA_KC - 1)
    onehot = (col_off[..., None] == np.arange(2 * NA_KC - 1)).astype(np.float32)
    tiles = jnp.einsum("hdo,qko->hdqk", rpb.astype(F32), onehot, precision=lax.Precision.HIGHEST)
    tiles = jnp.where(col_ok, tiles, -jnp.inf)
    masked = jnp.full((NA_HEADS, 1, w, w), -jnp.inf, F32)
    tiles = jnp.concatenate([tiles, masked], axis=1)
    tables = []
    for rq0, ws in _na_sub_geometry(rows):
        block_rows = []
        for qi in range(NA_SUB_ROWS):
            rq = rq0 + qi
            rs = min(max(rq - kr // 2, 0), rows - kr)
            picks = []
            for kj in range(NA_WIN_ROWS):
                rk = ws + kj
                picks.append(rk - rq + (NA_KR - 1) if rs <= rk < rs + kr else n_dr)
            block_rows.append(jnp.concatenate([tiles[:, dr] for dr in picks], axis=-1))
        tables.append(jnp.concatenate(block_rows, axis=1))
    t = jnp.stack(tables, axis=1)
    nq, nk = t.shape[2], t.shape[3]
    t = t.reshape(NA_HEADS // 2, 2, 5, nq, nk).transpose(0, 2, 1, 3, 4)
    return t.reshape(NA_HEADS // 2, 5, 2 * nq, nk)


def _na_kernel(rows, q_ref, k_ref, v_ref, tab_ref, o_ref, s_ref, p_ref, l_ref):
    n_sub = rows // NA_SUB_ROWS
    nq = NA_SUB_ROWS * GRID_W
    nk = NA_WIN_ROWS * GRID_W
    lane = lax.broadcasted_iota(jnp.int32, (nq, LANES), 1)

    def geometry(g):
        ws = min(max(NA_SUB_ROWS * g - NA_KR // 2, 0), rows - NA_WIN_ROWS)
        kind = g if g < 2 else (g - (n_sub - 5) if g >= n_sub - 2 else 2)
        return g * nq, ws * GRID_W, kind

    def stage_scores(g):
        q0, k0, kind = geometry(g)
        qg = q_ref[q0:q0 + nq, :]
        zero = jnp.zeros_like(qg)
        q2 = jnp.concatenate([jnp.where(lane < NA_HEAD_DIM, qg, zero),
                              jnp.where(lane >= NA_HEAD_DIM, qg, zero)], axis=0)
        s_ref[g % NA_STAGES] = _dot_nt(q2, k_ref[k0:k0 + nk, :]) + tab_ref[0, kind]

    def stage_softmax(g):
        s = s_ref[g % NA_STAGES]
        p = jnp.exp(s - jnp.max(s, axis=-1, keepdims=True))
        l_ref[g % NA_STAGES] = jnp.sum(p, axis=-1, keepdims=True)
        p_ref[g % NA_STAGES] = p.astype(BF16)

    def stage_pv(g):
        q0, k0, _ = geometry(g)
        pv = jnp.dot(p_ref[g % NA_STAGES], v_ref[k0:k0 + nk, :], preferred_element_type=F32)
        pv = pv / l_ref[g % NA_STAGES]
        o_ref[q0:q0 + nq, :] = jnp.where(lane < NA_HEAD_DIM, pv[:nq], pv[nq:]).astype(o_ref.dtype)

    for step in range(n_sub + 2):
        if step < n_sub:
            stage_scores(step)
        if 0 <= step - 1 < n_sub:
            stage_softmax(step - 1)
        if 0 <= step - 2 < n_sub:
            stage_pv(step - 2)


def _neighborhood_attention(qkv, batch, rpb):
    t = qkv.shape[1]
    s_len = t // batch
    rows = s_len // GRID_W
    assert rows % NA_SUB_ROWS == 0 and rows >= NA_WIN_ROWS + 2 * NA_SUB_ROWS and NA_KR % 2 == 0
    tables = _na_bias_tables(rpb, rows)
    hp = NA_HEADS // 2
    slab = lambda part: pl.BlockSpec((None, s_len, LANES), lambda pp, bb: (part * hp + pp, bb, 0))
    return pl.pallas_call(
        functools.partial(_na_kernel, rows),
        grid=(hp, batch),
        in_specs=[slab(0), slab(1), slab(2),
                  pl.BlockSpec((1,) + tables.shape[1:], lambda pp, bb: (pp, 0, 0, 0))],
        out_specs=slab(0),
        out_shape=jax.ShapeDtypeStruct((hp, t, LANES), BF16),
        scratch_shapes=[
            pltpu.VMEM((NA_STAGES,) + tables.shape[2:], F32),
            pltpu.VMEM((NA_STAGES,) + tables.shape[2:], BF16),
            pltpu.VMEM((NA_STAGES, tables.shape[2], 1), F32),
        ],
        compiler_params=pltpu.CompilerParams(
            dimension_semantics=("parallel", "arbitrary"), vmem_limit_bytes=VMEM_LIMIT),
        name="neighborhood_attention",
    )(qkv, qkv, qkv, tables)


def _post_kernel(final, o_ref, h_ref, p_ref, wo_ref, gf_ref, wg_ref, wu_ref, wd_ref,
                 gp_ref, wpg_ref, wpp_ref, gfin_ref, out_ref):
    o = jnp.concatenate([o_ref[c] for c in range(N_CB)], axis=-1)
    h = h_ref[...] + jnp.dot(o, wo_ref[...], preferred_element_type=F32)

    n = _rms(h, gf_ref[...], NORM_EPS).astype(BF16)
    acc = h
    for c in range(FFN_HIDDEN // FFN_TF):
        cols = slice(c * FFN_TF, (c + 1) * FFN_TF)
        gate = jnp.dot(n, wg_ref[:, cols], preferred_element_type=F32)
        up = jnp.dot(n, wu_ref[:, cols], preferred_element_type=F32)
        act = (gate * _sigmoid(gate) * up).astype(BF16)
        acc = acc + jnp.dot(act, wd_ref[cols, :], preferred_element_type=F32)
    h = acc

    n = _rms(h, gp_ref[...], NORM_EPS).astype(BF16)
    gate = _sigmoid(jnp.dot(n, wpg_ref[...], preferred_element_type=F32))
    proj = jnp.dot(p_ref[...].astype(BF16), wpp_ref[...], preferred_element_type=F32)
    h = h + gate * proj
    if final:
        h = _rms(h, gfin_ref[...], NORM_EPS)
    out_ref[...] = h


def _post_block(o, h, p, layer, wo, g_ffn, wg, wu, wd, g_ple, wpg, wpp, g_final, final):
    t, d = h.shape
    tok = lambda i: (i, 0)
    const = lambda i: (0, 0)
    resident = lambda a: pl.BlockSpec(a.shape, const, pipeline_mode=pl.Buffered(1))
    return pl.pallas_call(
        functools.partial(_post_kernel, final),
        grid=(t // POST_TM,),
        in_specs=[
            pl.BlockSpec((N_CB, POST_TM, LANES), lambda i: (0, i, 0)),
            pl.BlockSpec((POST_TM, d), tok),
            pl.BlockSpec((None, POST_TM, PLE_DIM), lambda i: (layer, i, 0)),
            resident(wo), resident(g_ffn), resident(wg), resident(wu), resident(wd),
            resident(g_ple), resident(wpg), resident(wpp), resident(g_final),
        ],
        out_specs=pl.BlockSpec((POST_TM, d), tok),
        out_shape=jax.ShapeDtypeStruct((t, d), F32),
        input_output_aliases={1: 0} if layer > 0 else {},
        compiler_params=pltpu.CompilerParams(
            dimension_semantics=("parallel",), vmem_limit_bytes=VMEM_LIMIT),
        name="post_block",
    )(o, h, p, wo, g_ffn, wg, wu, wd, g_ple, wpg, wpp, g_final)


def kernel(x, p, norm_mix, norm_ffn, norm_ple, norm_final, da_w_qkv, da_w_o, da_lambda_q1, da_lambda_k1, da_lambda_q2, da_lambda_k2, da_subln, na_w_qkv, na_b_qkv, na_rpb, na_w_o, ffn_w_gate, ffn_w_up, ffn_w_down, ple_w_proj, ple_w_gate):
    b, s_len, d = x.shape
    t = b * s_len
    row = lambda a: a.reshape(1, -1).astype(F32)
    q_scale = jnp.concatenate([jnp.full((d,), DA_HEAD_DIM ** -0.5, F32), jnp.ones((2 * d,), F32)])
    q_scale = q_scale.reshape(1, -1)
    p = p.reshape(DEPTH, t, PLE_DIM)
    h = x.reshape(t, d)
    for i in range(DEPTH):
        j = i // 2
        if i % 2 == 0:
            qkv = _qkv_proj(h, row(norm_mix[i]), da_w_qkv[j].astype(BF16),
                            jnp.zeros((1, 3 * d), F32), q_scale)
            mix = _diff_attention(qkv, b, da_lambda_q1[j], da_lambda_k1[j],
                                  da_lambda_q2[j], da_lambda_k2[j], da_subln[j], i)
            w_o = da_w_o[j]
        else:
            qkv = _qkv_proj(h, row(norm_mix[i]), na_w_qkv[j].astype(BF16),
                            row(na_b_qkv[j]), q_scale)
            mix = _neighborhood_attention(qkv, b, na_rpb[j])
            w_o = na_w_o[j]
        h = _post_block(mix, h, p, i, w_o.astype(BF16), row(norm_ffn[i]),
                        ffn_w_gate[i].astype(BF16), ffn_w_up[i].astype(BF16),
                        ffn_w_down[i].astype(BF16), row(norm_ple[i]),
                        ple_w_gate[i].astype(BF16), ple_w_proj[i].astype(BF16),
                        row(norm_final), final=(i == DEPTH - 1))
    return h.reshape(b, s_len, d)
```

```python
import functools
import math

import numpy as np
import jax
import jax.numpy as jnp
from jax import lax
from jax.experimental import pallas as pl
from jax.experimental.pallas import tpu as pltpu

F32 = jnp.float32
BF16 = jnp.bfloat16

D_MODEL = 1024
DEPTH = 4
PLE_DIM = 256
FFN_HIDDEN = 2816
DA_HEADS = 8
DA_HEAD_DIM = 64
NA_HEADS = 16
NA_HEAD_DIM = 64
GRID_W = 64
NA_KR = 8
NA_KC = 16
NORM_EPS = 1e-6
SUBLN_EPS = 1e-5

LANES = 128
MXU_WIDTH = 256
VMEM_LIMIT = 56 * 1024 * 1024
N_CB = D_MODEL // LANES

QKV_TM = 1024
QKV_TN = 1024
POST_TM = 512
FFN_TF = MXU_WIDTH
DA_TQ = MXU_WIDTH
DA_RC = 32
POS_SHIFT = 6
assert (1 << POS_SHIFT) == GRID_W == DA_HEAD_DIM
NA_SUB_ROWS = 2
NA_WIN_ROWS = NA_KR + 2
NA_STAGES = 3


def _rms(x, g, eps):
    return x * lax.rsqrt(jnp.mean(x * x, axis=-1, keepdims=True) + eps) * g


def _sigmoid(x):
    return 1.0 / (1.0 + jnp.exp(-x))


def _dot_nt(a, b):
    return lax.dot_general(a, b, (((1,), (1,)), ((), ())), preferred_element_type=F32)


def _qkv_kernel(x_ref, g_ref, w_ref, b_ref, cs_ref, o_ref, xn_ref):
    @pl.when(pl.program_id(1) == 0)
    def _():
        xn_ref[...] = _rms(x_ref[...], g_ref[...], NORM_EPS).astype(BF16)

    acc = jnp.dot(xn_ref[...], w_ref[...], preferred_element_type=F32)
    res = ((acc + b_ref[...]) * cs_ref[...]).astype(o_ref.dtype)
    for c in range(QKV_TN // LANES):
        o_ref[c] = res[:, c * LANES:(c + 1) * LANES]


def _qkv_proj(h, g, w, b, cs):
    t, d = h.shape
    n = w.shape[1]
    cb = QKV_TN // LANES
    return pl.pallas_call(
        _qkv_kernel,
        grid=(t // QKV_TM, n // QKV_TN),
        in_specs=[
            pl.BlockSpec((QKV_TM, d), lambda i, j: (i, 0)),
            pl.BlockSpec((1, d), lambda i, j: (0, 0)),
            pl.BlockSpec((d, QKV_TN), lambda i, j: (0, j)),
            pl.BlockSpec((1, QKV_TN), lambda i, j: (0, j)),
            pl.BlockSpec((1, QKV_TN), lambda i, j: (0, j)),
        ],
        out_specs=pl.BlockSpec((cb, QKV_TM, LANES), lambda i, j: (j, i, 0)),
        out_shape=jax.ShapeDtypeStruct((n // LANES, t, LANES), BF16),
        scratch_shapes=[pltpu.VMEM((QKV_TM, d), BF16)],
        compiler_params=pltpu.CompilerParams(
            dimension_semantics=("parallel", "arbitrary"), vmem_limit_bytes=VMEM_LIMIT),
        name="qkv_proj",
    )(h, g, w, b, cs)


def _da_key_features(s_len):
    pos = np.arange(s_len)
    f = np.zeros((s_len, LANES), np.float32)
    for base in (0, DA_HEAD_DIM):
        f[:, base + 0] = pos // GRID_W
        f[:, base + 1] = pos % GRID_W
        f[:, base + 2] = 1.0
        f[:, base + 3] = 1.0
    return jnp.asarray(f, BF16)


def _da_diag_correction():
    r = np.arange(DA_TQ)
    return jnp.asarray(-2.0 * np.maximum(r[None, :] - r[:, None], 0), F32)


def _da_kernel(lam_init, nb, slopes_ref, qc_ref, qn_ref, k_ref, v_ref, kfeat_ref, corr_ref,
               lq1_ref, lk1_ref, lq2_ref, lk2_ref, sub_ref, o_ref,
               kaug_ref, vaug_ref, qv_ref, sa_ref, sb_ref, pa_ref, pb_ref):
    h = pl.program_id(1)
    t = pl.program_id(2)
    tq = DA_TQ
    slope = slopes_ref[h]
    corr = slope * corr_ref[...]

    def scores(q, i_tile, slot, s_ref):
        lane = lax.broadcasted_iota(jnp.int32, q.shape, 1)
        fidx = jnp.bitwise_and(lane, DA_HEAD_DIM - 1)
        pos = i_tile * tq + lax.broadcasted_iota(jnp.int32, q.shape, 0)
        pos_hi = jnp.right_shift(pos, POS_SHIFT).astype(F32)
        pos_lo = jnp.bitwise_and(pos, GRID_W - 1).astype(F32)
        feat = jnp.where(fidx == 0, GRID_W * slope,
                         jnp.where(fidx == 1, slope,
                                   jnp.where(fidx == 2, -GRID_W * slope * pos_hi,
                                             jnp.where(fidx == 3, -slope * pos_lo, 0.0))))
        feat_l = feat.astype(BF16)
        feat_r = (-feat).astype(BF16)
        for m in range(2):
            is_data = (lane < DA_HEAD_DIM) if m == 0 else (lane >= DA_HEAD_DIM)
            qv_ref[slot, m, 0] = jnp.where(is_data, q, feat_l)
            qv_ref[slot, m, 1] = jnp.where(is_data, q, feat_r)

        def block(m, j):
            side = jnp.where(j > i_tile, 1, 0)
            s_ref[m, j] = _dot_nt(qv_ref[slot, m, side], kaug_ref[m, j * tq:(j + 1) * tq, :])
            if j == nb - 1:
                s_ref[m, i_tile] = s_ref[m, i_tile] + corr

        return [functools.partial(block, m, j) for m in range(2) for j in range(nb)]

    def softmax_chunk(s_ref, p_ref, m, rc):
        rows = slice(rc * DA_RC, (rc + 1) * DA_RC)
        m_el = s_ref[m, 0, rows, :]
        for j in range(1, nb):
            m_el = jnp.maximum(m_el, s_ref[m, j, rows, :])
        mb = jnp.broadcast_to(jnp.max(m_el, axis=-1, keepdims=True), m_el.shape)
        for j in range(nb):
            p_ref[m, rows, j * tq:(j + 1) * tq] = jnp.exp(s_ref[m, j, rows, :] - mb).astype(BF16)

    def softmax_pv(s_ref, p_ref, mxu_tasks):
        n_chunks = tq // DA_RC
        per_chunk = len(mxu_tasks) // (2 * n_chunks)
        outs = []
        for m in range(2):
            for rc in range(n_chunks):
                softmax_chunk(s_ref, p_ref, m, rc)
                for _ in range(per_chunk):
                    mxu_tasks.pop(0)()
            res = jnp.dot(p_ref[m], vaug_ref[...], preferred_element_type=F32)
            outs.append(res[:, :LANES] / res[:, LANES:])
        while mxu_tasks:
            mxu_tasks.pop(0)()
        lam = (jnp.exp(jnp.sum(lq1_ref[...] * lk1_ref[...], axis=-1, keepdims=True))
               - jnp.exp(jnp.sum(lq2_ref[...] * lk2_ref[...], axis=-1, keepdims=True))
               + lam_init)
        o = outs[0] - lam * outs[1]
        o = _rms(o, sub_ref[...], SUBLN_EPS) * (1.0 - lam_init)
        return o.astype(o_ref.dtype)

    @pl.when(t == 0)
    def _():
        k = k_ref[...]
        kf = kfeat_ref[...]
        lane_k = lax.broadcasted_iota(jnp.int32, k.shape, 1)
        kaug_ref[0] = jnp.where(lane_k < DA_HEAD_DIM, k, kf)
        kaug_ref[1] = jnp.where(lane_k >= DA_HEAD_DIM, k, kf)
        vaug_ref[:, :LANES] = v_ref[...]
        vaug_ref[:, LANES:] = jnp.ones(v_ref.shape, BF16)
        for task in scores(qc_ref[:tq, :], 2 * t, 0, sa_ref):
            task()

    tasks_b = scores(qc_ref[tq:, :], 2 * t + 1, 1, sb_ref)
    o_ref[:tq, :] = softmax_pv(sa_ref, pa_ref, tasks_b)
    tasks_n = scores(qn_ref[...], jnp.minimum(2 * t + 2, nb - 1), 2, sa_ref)
    o_ref[tq:, :] = softmax_pv(sb_ref, pb_ref, tasks_n)


def _diff_attention(qkv, batch, lq1, lk1, lq2, lk2, subln, layer_idx):
    t = qkv.shape[1]
    s_len = t // batch
    nb = s_len // DA_TQ
    lam_init = 0.8 - 0.6 * math.exp(-0.3 * layer_idx)
    slopes = [2.0 ** (-8.0 * (hh + 1) / DA_HEADS) for hh in range(DA_HEADS)]
    for sl in slopes:
        assert float(np.asarray(sl * GRID_W * (GRID_W - 1), dtype=BF16)) == sl * GRID_W * (GRID_W - 1)
    assert s_len <= GRID_W * GRID_W and nb % 2 == 0
    steps = nb // 2
    vec = lambda a: a.reshape(1, -1).astype(F32)
    small = pl.BlockSpec((1, DA_HEAD_DIM), lambda bb, hh, ii: (0, 0))
    return pl.pallas_call(
        functools.partial(_da_kernel, lam_init, nb),
        grid=(batch, DA_HEADS, steps),
        in_specs=[
            pl.BlockSpec(memory_space=pltpu.SMEM),
            pl.BlockSpec((None, 2 * DA_TQ, LANES), lambda bb, hh, ii: (hh, bb * steps + ii, 0)),
            pl.BlockSpec((None, DA_TQ, LANES),
                         lambda bb, hh, ii: (hh, bb * nb + jnp.minimum(2 * ii + 2, nb - 1), 0)),
            pl.BlockSpec((None, s_len, LANES), lambda bb, hh, ii: (DA_HEADS + hh, bb, 0)),
            pl.BlockSpec((None, s_len, LANES), lambda bb, hh, ii: (2 * DA_HEADS + hh, bb, 0)),
            pl.BlockSpec((s_len, LANES), lambda bb, hh, ii: (0, 0)),
            pl.BlockSpec((DA_TQ, DA_TQ), lambda bb, hh, ii: (0, 0)),
            small, small, small, small,
            pl.BlockSpec((1, LANES), lambda bb, hh, ii: (0, 0)),
        ],
        out_specs=pl.BlockSpec((None, 2 * DA_TQ, LANES), lambda bb, hh, ii: (hh, bb * steps + ii, 0)),
        out_shape=jax.ShapeDtypeStruct((DA_HEADS, t, LANES), BF16),
        scratch_shapes=[
            pltpu.VMEM((2, s_len, LANES), BF16),
            pltpu.VMEM((s_len, 2 * LANES), BF16),
            pltpu.VMEM((3, 2, 2, DA_TQ, LANES), BF16),
            pltpu.VMEM((2, nb, DA_TQ, DA_TQ), F32),
            pltpu.VMEM((2, nb, DA_TQ, DA_TQ), F32),
            pltpu.VMEM((2, DA_TQ, s_len), BF16),
            pltpu.VMEM((2, DA_TQ, s_len), BF16),
        ],
        compiler_params=pltpu.CompilerParams(
            dimension_semantics=("parallel", "parallel", "arbitrary"), vmem_limit_bytes=VMEM_LIMIT),
        name="diff_attention",
    )(jnp.asarray(slopes, F32), qkv, qkv, qkv, qkv, _da_key_features(s_len), _da_diag_correction(),
      vec(lq1), vec(lk1), vec(lq2), vec(lk2), vec(subln))


def _na_sub_geometry(rows):
    last = rows - NA_WIN_ROWS
    return ((0, 0), (NA_SUB_ROWS, 0), (2 * NA_SUB_ROWS, 0),
            (rows - 2 * NA_SUB_ROWS, last), (rows - NA_SUB_ROWS, last))


def _na_bias_tables(rpb, rows):
    w = GRID_W
    kr = min(NA_KR, rows)
    n_dr = 2 * NA_KR - 1
    cq, ck = np.arange(w)[:, None], np.arange(w)[None, :]
    cstart = np.clip(cq - NA_KC // 2, 0, w - NA_KC)
    col_ok = (ck >= cstart) & (ck < cstart + NA_KC)
    col_off = np.clip(ck - cq, -(NA_KC - 1), NA_KC - 1) + (NA_KC - 1)
    onehot = (col_off[..., None] == np.arange(2 * NA_KC - 1)).astype(np.float32)
    tiles = jnp.einsum("hdo,qko->hdqk", rpb.astype(F32), onehot, precision=lax.Precision.HIGHEST)
    tiles = jnp.where(col_ok, tiles, -jnp.inf)
    masked = jnp.full((NA_HEADS, 1, w, w), -jnp.inf, F32)
    tiles = jnp.concatenate([tiles, masked], axis=1)
    tables = []
    for rq0, ws in _na_sub_geometry(rows):
        block_rows = []
        for qi in range(NA_SUB_ROWS):
            rq = rq0 + qi
            rs = min(max(rq - kr // 2, 0), rows - kr)
            picks = []
            for kj in range(NA_WIN_ROWS):
                rk = ws + kj
                picks.append(rk - rq + (NA_KR - 1) if rs <= rk < rs + kr else n_dr)
            block_rows.append(jnp.concatenate([tiles[:, dr] for dr in picks], axis=-1))
        tables.append(jnp.concatenate(block_rows, axis=1))
    t = jnp.stack(tables, axis=1)
    nq, nk = t.shape[2], t.shape[3]
    t = t.reshape(NA_HEADS // 2, 2, 5, nq, nk).transpose(0, 2, 1, 3, 4)
    return t.reshape(NA_HEADS // 2, 5, 2 * nq, nk)


def _na_kernel(rows, q_ref, k_ref, v_ref, tab_ref, o_ref, s_ref, p_ref, vaug_ref):
    n_sub = rows // NA_SUB_ROWS
    nq = NA_SUB_ROWS * GRID_W
    nk = NA_WIN_ROWS * GRID_W
    lane = lax.broadcasted_iota(jnp.int32, (nq, LANES), 1)

    def geometry(g):
        ws = min(max(NA_SUB_ROWS * g - NA_KR // 2, 0), rows - NA_WIN_ROWS)
        kind = g if g < 2 else (g - (n_sub - 5) if g >= n_sub - 2 else 2)
        return g * nq, ws * GRID_W, kind

    def stage_scores(g):
        q0, k0, kind = geometry(g)
        qg = q_ref[q0:q0 + nq, :]
        zero = jnp.zeros_like(qg)
        q2 = jnp.concatenate([jnp.where(lane < NA_HEAD_DIM, qg, zero),
                              jnp.where(lane >= NA_HEAD_DIM, qg, zero)], axis=0)
        s_ref[g % NA_STAGES] = _dot_nt(q2, k_ref[k0:k0 + nk, :]) + tab_ref[0, kind]

    def stage_softmax(g):
        s = s_ref[g % NA_STAGES]
        p_ref[g % NA_STAGES] = jnp.exp(s - jnp.max(s, axis=-1, keepdims=True)).astype(BF16)

    def stage_pv(g):
        q0, k0, _ = geometry(g)
        res = jnp.dot(p_ref[g % NA_STAGES], vaug_ref[k0:k0 + nk, :], preferred_element_type=F32)
        pv = res[:, :LANES] / res[:, LANES:]
        o_ref[q0:q0 + nq, :] = jnp.where(lane < NA_HEAD_DIM, pv[:nq], pv[nq:]).astype(o_ref.dtype)

    vaug_ref[:, :LANES] = v_ref[...]
    vaug_ref[:, LANES:] = jnp.ones(v_ref.shape, BF16)

    for step in range(n_sub + 2):
        if step < n_sub:
            stage_scores(step)
        if 0 <= step - 1 < n_sub:
            stage_softmax(step - 1)
        if 0 <= step - 2 < n_sub:
            stage_pv(step - 2)


def _neighborhood_attention(qkv, batch, rpb):
    t = qkv.shape[1]
    s_len = t // batch
    rows = s_len // GRID_W
    assert rows % NA_SUB_ROWS == 0 and rows >= NA_WIN_ROWS + 2 * NA_SUB_ROWS and NA_KR % 2 == 0
    tables = _na_bias_tables(rpb, rows)
    hp = NA_HEADS // 2
    slab = lambda part: pl.BlockSpec((None, s_len, LANES), lambda pp, bb: (part * hp + pp, bb, 0))
    return pl.pallas_call(
        functools.partial(_na_kernel, rows),
        grid=(hp, batch),
        in_specs=[slab(0), slab(1), slab(2),
                  pl.BlockSpec((1,) + tables.shape[1:], lambda pp, bb: (pp, 0, 0, 0))],
        out_specs=slab(0),
        out_shape=jax.ShapeDtypeStruct((hp, t, LANES), BF16),
        scratch_shapes=[
            pltpu.VMEM((NA_STAGES,) + tables.shape[2:], F32),
            pltpu.VMEM((NA_STAGES,) + tables.shape[2:], BF16),
            pltpu.VMEM((s_len, 2 * LANES), BF16),
        ],
        compiler_params=pltpu.CompilerParams(
            dimension_semantics=("parallel", "arbitrary"), vmem_limit_bytes=VMEM_LIMIT),
        name="neighborhood_attention",
    )(qkv, qkv, qkv, tables)


def _post_kernel(final, o_ref, h_ref, p_ref, wo_ref, gf_ref, wg_ref, wu_ref, wd_ref,
                 gp_ref, wpg_ref, wpp_ref, gfin_ref, out_ref):
    o = jnp.concatenate([o_ref[c] for c in range(N_CB)], axis=-1)
    h = h_ref[...] + jnp.dot(o, wo_ref[...], preferred_element_type=F32)

    n = _rms(h, gf_ref[...], NORM_EPS).astype(BF16)
    acc = h
    for c in range(FFN_HIDDEN // FFN_TF):
        cols = slice(c * FFN_TF, (c + 1) * FFN_TF)
        gate = jnp.dot(n, wg_ref[:, cols], preferred_element_type=F32)
        up = jnp.dot(n, wu_ref[:, cols], preferred_element_type=F32)
        act = (gate * _sigmoid(gate) * up).astype(BF16)
        acc = acc + jnp.dot(act, wd_ref[cols, :], preferred_element_type=F32)
    h = acc

    n = _rms(h, gp_ref[...], NORM_EPS).astype(BF16)
    gate = _sigmoid(jnp.dot(n, wpg_ref[...], preferred_element_type=F32))
    proj = jnp.dot(p_ref[...].astype(BF16), wpp_ref[...], preferred_element_type=F32)
    h = h + gate * proj
    if final:
        h = _rms(h, gfin_ref[...], NORM_EPS)
    out_ref[...] = h


def _post_block(o, h, p, layer, wo, g_ffn, wg, wu, wd, g_ple, wpg, wpp, g_final, final):
    t, d = h.shape
    tok = lambda i: (i, 0)
    const = lambda i: (0, 0)
    resident = lambda a: pl.BlockSpec(a.shape, const, pipeline_mode=pl.Buffered(1))
    return pl.pallas_call(
        functools.partial(_post_kernel, final),
        grid=(t // POST_TM,),
        in_specs=[
            pl.BlockSpec((N_CB, POST_TM, LANES), lambda i: (0, i, 0)),
            pl.BlockSpec((POST_TM, d), tok),
            pl.BlockSpec((None, POST_TM, PLE_DIM), lambda i: (layer, i, 0)),
            resident(wo), resident(g_ffn), resident(wg), resident(wu), resident(wd),
            resident(g_ple), resident(wpg), resident(wpp), resident(g_final),
        ],
        out_specs=pl.BlockSpec((POST_TM, d), tok),
        out_shape=jax.ShapeDtypeStruct((t, d), F32),
        input_output_aliases={1: 0} if layer > 0 else {},
        compiler_params=pltpu.CompilerParams(
            dimension_semantics=("parallel",), vmem_limit_bytes=VMEM_LIMIT),
        name="post_block",
    )(o, h, p, wo, g_ffn, wg, wu, wd, g_ple, wpg, wpp, g_final)


def kernel(x, p, norm_mix, norm_ffn, norm_ple, norm_final, da_w_qkv, da_w_o, da_lambda_q1, da_lambda_k1, da_lambda_q2, da_lambda_k2, da_subln, na_w_qkv, na_b_qkv, na_rpb, na_w_o, ffn_w_gate, ffn_w_up, ffn_w_down, ple_w_proj, ple_w_gate):
    b, s_len, d = x.shape
    t = b * s_len
    row = lambda a: a.reshape(1, -1).astype(F32)
    q_scale = jnp.concatenate([jnp.full((d,), DA_HEAD_DIM ** -0.5, F32), jnp.ones((2 * d,), F32)])
    q_scale = q_scale.reshape(1, -1)
    p = p.reshape(DEPTH, t, PLE_DIM)
    h = x.reshape(t, d)
    for i in range(DEPTH):
        j = i // 2
        if i % 2 == 0:
            qkv = _qkv_proj(h, row(norm_mix[i]), da_w_qkv[j].astype(BF16),
                            jnp.zeros((1, 3 * d), F32), q_scale)
            mix = _diff_attention(qkv, b, da_lambda_q1[j], da_lambda_k1[j],
                                  da_lambda_q2[j], da_lambda_k2[j], da_subln[j], i)
            w_o = da_w_o[j]
        else:
            qkv = _qkv_proj(h, row(norm_mix[i]), na_w_qkv[j].astype(BF16),
                            row(na_b_qkv[j]), q_scale)
            mix = _neighborhood_attention(qkv, b, na_rpb[j])
            w_o = na_w_o[j]
        h = _post_block(mix, h, p, i, w_o.astype(BF16), row(norm_ffn[i]),
                        ffn_w_gate[i].astype(BF16), ffn_w_up[i].astype(BF16),
                        ffn_w_down[i].astype(BF16), row(norm_ple[i]),
                        ple_w_gate[i].astype(BF16), ple_w_proj[i].astype(BF16),
                        row(norm_final), final=(i == DEPTH - 1))
    return h.reshape(b, s_len, d)
```

```python
import functools
import math

import numpy as np
import jax
import jax.numpy as jnp
from jax import lax
from jax.experimental import pallas as pl
from jax.experimental.pallas import tpu as pltpu

F32 = jnp.float32
BF16 = jnp.bfloat16

D_MODEL = 1024
DEPTH = 4
PLE_DIM = 256
FFN_HIDDEN = 2816
DA_HEADS = 8
DA_HEAD_DIM = 64
NA_HEADS = 16
NA_HEAD_DIM = 64
GRID_W = 64
NA_KR = 8
NA_KC = 16
NORM_EPS = 1e-6
SUBLN_EPS = 1e-5

LANES = 128
MXU_WIDTH = 256
VMEM_LIMIT = 56 * 1024 * 1024
N_CB = D_MODEL // LANES

QKV_TM = 1024
QKV_TN = 1024
POST_TM = 512
FFN_TF = MXU_WIDTH
DA_TQ = MXU_WIDTH
DA_RC = 32
DA_TILES = 4
POS_SHIFT = 6
assert (1 << POS_SHIFT) == GRID_W == DA_HEAD_DIM
NA_SUB_ROWS = 2
NA_WIN_ROWS = NA_KR + 2
NA_STAGES = 3


def _rms(x, g, eps):
    return x * lax.rsqrt(jnp.mean(x * x, axis=-1, keepdims=True) + eps) * g


def _sigmoid(x):
    return 1.0 / (1.0 + jnp.exp(-x))


def _dot_nt(a, b):
    return lax.dot_general(a, b, (((1,), (1,)), ((), ())), preferred_element_type=F32)


def _qkv_kernel(x_ref, g_ref, w_ref, b_ref, cs_ref, o_ref, xn_ref):
    @pl.when(pl.program_id(1) == 0)
    def _():
        xn_ref[...] = _rms(x_ref[...], g_ref[...], NORM_EPS).astype(BF16)

    acc = jnp.dot(xn_ref[...], w_ref[...], preferred_element_type=F32)
    res = ((acc + b_ref[...]) * cs_ref[...]).astype(o_ref.dtype)
    for c in range(QKV_TN // LANES):
        o_ref[c] = res[:, c * LANES:(c + 1) * LANES]


def _qkv_proj(h, g, w, b, cs):
    t, d = h.shape
    n = w.shape[1]
    cb = QKV_TN // LANES
    return pl.pallas_call(
        _qkv_kernel,
        grid=(t // QKV_TM, n // QKV_TN),
        in_specs=[
            pl.BlockSpec((QKV_TM, d), lambda i, j: (i, 0)),
            pl.BlockSpec((1, d), lambda i, j: (0, 0)),
            pl.BlockSpec((d, QKV_TN), lambda i, j: (0, j)),
            pl.BlockSpec((1, QKV_TN), lambda i, j: (0, j)),
            pl.BlockSpec((1, QKV_TN), lambda i, j: (0, j)),
        ],
        out_specs=pl.BlockSpec((cb, QKV_TM, LANES), lambda i, j: (j, i, 0)),
        out_shape=jax.ShapeDtypeStruct((n // LANES, t, LANES), BF16),
        scratch_shapes=[pltpu.VMEM((QKV_TM, d), BF16)],
        compiler_params=pltpu.CompilerParams(
            dimension_semantics=("parallel", "arbitrary"), vmem_limit_bytes=VMEM_LIMIT),
        name="qkv_proj",
    )(h, g, w, b, cs)


def _da_key_features(s_len):
    pos = np.arange(s_len)
    f = np.zeros((s_len, LANES), np.float32)
    for base in (0, DA_HEAD_DIM):
        f[:, base + 0] = pos // GRID_W
        f[:, base + 1] = pos % GRID_W
        f[:, base + 2] = 1.0
        f[:, base + 3] = 1.0
    return jnp.asarray(f, BF16)


def _da_diag_correction():
    r = np.arange(DA_TQ)
    return jnp.asarray(-2.0 * np.maximum(r[None, :] - r[:, None], 0), F32)


def _da_kernel(lam_init, nb, slopes_ref, qc_ref, qn_ref, k_ref, v_ref, kfeat_ref, corr_ref,
               lq1_ref, lk1_ref, lq2_ref, lk2_ref, sub_ref, o_ref,
               kaug_ref, vaug_ref, qv_ref, sa_ref, sb_ref, pa_ref, pb_ref):
    h = pl.program_id(1)
    t = pl.program_id(2)
    tq = DA_TQ
    slope = slopes_ref[h]
    corr = slope * corr_ref[...]

    def scores(q, i_tile, slot, s_ref):
        lane = lax.broadcasted_iota(jnp.int32, q.shape, 1)
        fidx = jnp.bitwise_and(lane, DA_HEAD_DIM - 1)
        pos = i_tile * tq + lax.broadcasted_iota(jnp.int32, q.shape, 0)
        pos_hi = jnp.right_shift(pos, POS_SHIFT).astype(F32)
        pos_lo = jnp.bitwise_and(pos, GRID_W - 1).astype(F32)
        feat = jnp.where(fidx == 0, GRID_W * slope,
                         jnp.where(fidx == 1, slope,
                                   jnp.where(fidx == 2, -GRID_W * slope * pos_hi,
                                             jnp.where(fidx == 3, -slope * pos_lo, 0.0))))
        feat_l = feat.astype(BF16)
        feat_r = (-feat).astype(BF16)
        for m in range(2):
            is_data = (lane < DA_HEAD_DIM) if m == 0 else (lane >= DA_HEAD_DIM)
            qv_ref[slot, m, 0] = jnp.where(is_data, q, feat_l)
            qv_ref[slot, m, 1] = jnp.where(is_data, q, feat_r)

        def block(m, j):
            side = jnp.where(j > i_tile, 1, 0)
            s_ref[m, j] = _dot_nt(qv_ref[slot, m, side], kaug_ref[m, j * tq:(j + 1) * tq, :])
            if j == nb - 1:
                s_ref[m, i_tile] = s_ref[m, i_tile] + corr

        return [functools.partial(block, m, j) for m in range(2) for j in range(nb)]

    def softmax_chunk(s_ref, p_ref, m, rc):
        rows = slice(rc * DA_RC, (rc + 1) * DA_RC)
        m_el = s_ref[m, 0, rows, :]
        for j in range(1, nb):
            m_el = jnp.maximum(m_el, s_ref[m, j, rows, :])
        mb = jnp.broadcast_to(jnp.max(m_el, axis=-1, keepdims=True), m_el.shape)
        for j in range(nb):
            p_ref[m, rows, j * tq:(j + 1) * tq] = jnp.exp(s_ref[m, j, rows, :] - mb).astype(BF16)

    def softmax_pv(s_ref, p_ref, mxu_tasks):
        n_chunks = tq // DA_RC
        per_chunk = len(mxu_tasks) // (2 * n_chunks)
        outs = []
        for m in range(2):
            for rc in range(n_chunks):
                softmax_chunk(s_ref, p_ref, m, rc)
                for _ in range(per_chunk):
                    mxu_tasks.pop(0)()
            res = jnp.dot(p_ref[m], vaug_ref[...], preferred_element_type=F32)
            outs.append(res[:, :LANES] / res[:, LANES:])
        while mxu_tasks:
            mxu_tasks.pop(0)()
        lam = (jnp.exp(jnp.sum(lq1_ref[...] * lk1_ref[...], axis=-1, keepdims=True))
               - jnp.exp(jnp.sum(lq2_ref[...] * lk2_ref[...], axis=-1, keepdims=True))
               + lam_init)
        o = outs[0] - lam * outs[1]
        o = _rms(o, sub_ref[...], SUBLN_EPS) * (1.0 - lam_init)
        return o.astype(o_ref.dtype)

    @pl.when(t == 0)
    def _():
        k = k_ref[...]
        kf = kfeat_ref[...]
        lane_k = lax.broadcasted_iota(jnp.int32, k.shape, 1)
        kaug_ref[0] = jnp.where(lane_k < DA_HEAD_DIM, k, kf)
        kaug_ref[1] = jnp.where(lane_k >= DA_HEAD_DIM, k, kf)
        vaug_ref[:, :LANES] = v_ref[...]
        vaug_ref[:, LANES:] = jnp.ones(v_ref.shape, BF16)
        for task in scores(qc_ref[:tq, :], DA_TILES * t, 0, sa_ref):
            task()

    s_bufs, p_bufs = (sa_ref, sb_ref), (pa_ref, pb_ref)
    for n in range(DA_TILES):
        if n + 1 < DA_TILES:
            q_next, i_next = qc_ref[(n + 1) * tq:(n + 2) * tq, :], DA_TILES * t + n + 1
        else:
            q_next, i_next = qn_ref[...], jnp.minimum(DA_TILES * (t + 1), nb - 1)
        tasks = scores(q_next, i_next, n + 1, s_bufs[(n + 1) % 2])
        o_ref[n * tq:(n + 1) * tq, :] = softmax_pv(s_bufs[n % 2], p_bufs[n % 2], tasks)


def _diff_attention(qkv, batch, lq1, lk1, lq2, lk2, subln, layer_idx):
    t = qkv.shape[1]
    s_len = t // batch
    nb = s_len // DA_TQ
    lam_init = 0.8 - 0.6 * math.exp(-0.3 * layer_idx)
    slopes = [2.0 ** (-8.0 * (hh + 1) / DA_HEADS) for hh in range(DA_HEADS)]
    for sl in slopes:
        assert float(np.asarray(sl * GRID_W * (GRID_W - 1), dtype=BF16)) == sl * GRID_W * (GRID_W - 1)
    assert s_len <= GRID_W * GRID_W and nb % DA_TILES == 0 and DA_TILES % 2 == 0
    steps = nb // DA_TILES
    step_rows = DA_TILES * DA_TQ
    vec = lambda a: a.reshape(1, -1).astype(F32)
    small = pl.BlockSpec((1, DA_HEAD_DIM), lambda bb, hh, ii: (0, 0))
    return pl.pallas_call(
        functools.partial(_da_kernel, lam_init, nb),
        grid=(batch, DA_HEADS, steps),
        in_specs=[
            pl.BlockSpec(memory_space=pltpu.SMEM),
            pl.BlockSpec((None, step_rows, LANES), lambda bb, hh, ii: (hh, bb * steps + ii, 0)),
            pl.BlockSpec((None, DA_TQ, LANES),
                         lambda bb, hh, ii: (hh, bb * nb + jnp.minimum(DA_TILES * (ii + 1), nb - 1), 0)),
            pl.BlockSpec((None, s_len, LANES), lambda bb, hh, ii: (DA_HEADS + hh, bb, 0)),
            pl.BlockSpec((None, s_len, LANES), lambda bb, hh, ii: (2 * DA_HEADS + hh, bb, 0)),
            pl.BlockSpec((s_len, LANES), lambda bb, hh, ii: (0, 0)),
            pl.BlockSpec((DA_TQ, DA_TQ), lambda bb, hh, ii: (0, 0)),
            small, small, small, small,
            pl.BlockSpec((1, LANES), lambda bb, hh, ii: (0, 0)),
        ],
        out_specs=pl.BlockSpec((None, step_rows, LANES), lambda bb, hh, ii: (hh, bb * steps + ii, 0)),
        out_shape=jax.ShapeDtypeStruct((DA_HEADS, t, LANES), BF16),
        scratch_shapes=[
            pltpu.VMEM((2, s_len, LANES), BF16),
            pltpu.VMEM((s_len, 2 * LANES), BF16),
            pltpu.VMEM((DA_TILES + 1, 2, 2, DA_TQ, LANES), BF16),
            pltpu.VMEM((2, nb, DA_TQ, DA_TQ), F32),
            pltpu.VMEM((2, nb, DA_TQ, DA_TQ), F32),
            pltpu.VMEM((2, DA_TQ, s_len), BF16),
            pltpu.VMEM((2, DA_TQ, s_len), BF16),
        ],
        compiler_params=pltpu.CompilerParams(
            dimension_semantics=("parallel", "parallel", "arbitrary"), vmem_limit_bytes=VMEM_LIMIT),
        name="diff_attention",
    )(jnp.asarray(slopes, F32), qkv, qkv, qkv, qkv, _da_key_features(s_len), _da_diag_correction(),
      vec(lq1), vec(lk1), vec(lq2), vec(lk2), vec(subln))


def _na_sub_geometry(rows):
    last = rows - NA_WIN_ROWS
    return ((0, 0), (NA_SUB_ROWS, 0), (2 * NA_SUB_ROWS, 0),
            (rows - 2 * NA_SUB_ROWS, last), (rows - NA_SUB_ROWS, last))


def _na_bias_tables(rpb, rows):
    w = GRID_W
    kr = min(NA_KR, rows)
    n_dr = 2 * NA_KR - 1
    cq, ck = np.arange(w)[:, None], np.arange(w)[None, :]
    cstart = np.clip(cq - NA_KC // 2, 0, w - NA_KC)
    col_ok = (ck >= cstart) & (ck < cstart + NA_KC)
    col_off = np.clip(ck - cq, -(NA_KC - 1), NA_KC - 1) + (NA_KC - 1)
    onehot = (col_off[..., None] == np.arange(2 * NA_KC - 1)).astype(np.float32)
    tiles = jnp.einsum("hdo,qko->hdqk", rpb.astype(F32), onehot, precision=lax.Precision.HIGHEST)
    tiles = jnp.where(col_ok, tiles, -jnp.inf)
    masked = jnp.full((NA_HEADS, 1, w, w), -jnp.inf, F32)
    tiles = jnp.concatenate([tiles, masked], axis=1)
    tables = []
    for rq0, ws in _na_sub_geometry(rows):
        block_rows = []
        for qi in range(NA_SUB_ROWS):
            rq = rq0 + qi
            rs = min(max(rq - kr // 2, 0), rows - kr)
            picks = []
            for kj in range(NA_WIN_ROWS):
                rk = ws + kj
                picks.append(rk - rq + (NA_KR - 1) if rs <= rk < rs + kr else n_dr)
            block_rows.append(jnp.concatenate([tiles[:, dr] for dr in picks], axis=-1))
        tables.append(jnp.concatenate(block_rows, axis=1))
    t = jnp.stack(tables, axis=1)
    nq, nk = t.shape[2], t.shape[3]
    t = t.reshape(NA_HEADS // 2, 2, 5, nq, nk).transpose(0, 2, 1, 3, 4)
    return t.reshape(NA_HEADS // 2, 5, 2 * nq, nk)


def _na_kernel(rows, q_ref, k_ref, v_ref, tab_ref, o_ref, s_ref, p_ref, vaug_ref):
    n_sub = rows // NA_SUB_ROWS
    nq = NA_SUB_ROWS * GRID_W
    nk = NA_WIN_ROWS * GRID_W
    lane = lax.broadcasted_iota(jnp.int32, (nq, LANES), 1)

    def geometry(g):
        ws = min(max(NA_SUB_ROWS * g - NA_KR // 2, 0), rows - NA_WIN_ROWS)
        kind = g if g < 2 else (g - (n_sub - 5) if g >= n_sub - 2 else 2)
        return g * nq, ws * GRID_W, kind

    def stage_scores(g):
        q0, k0, kind = geometry(g)
        qg = q_ref[q0:q0 + nq, :]
        zero = jnp.zeros_like(qg)
        q2 = jnp.concatenate([jnp.where(lane < NA_HEAD_DIM, qg, zero),
                              jnp.where(lane >= NA_HEAD_DIM, qg, zero)], axis=0)
        s_ref[g % NA_STAGES] = _dot_nt(q2, k_ref[k0:k0 + nk, :]) + tab_ref[0, kind]

    def stage_softmax(g):
        s = s_ref[g % NA_STAGES]
        p_ref[g % NA_STAGES] = jnp.exp(s - jnp.max(s, axis=-1, keepdims=True)).astype(BF16)

    def stage_pv(g):
        q0, k0, _ = geometry(g)
        res = jnp.dot(p_ref[g % NA_STAGES], vaug_ref[k0:k0 + nk, :], preferred_element_type=F32)
        pv = res[:, :LANES] / res[:, LANES:]
        o_ref[q0:q0 + nq, :] = jnp.where(lane < NA_HEAD_DIM, pv[:nq], pv[nq:]).astype(o_ref.dtype)

    vaug_ref[:, :LANES] = v_ref[...]
    vaug_ref[:, LANES:] = jnp.ones(v_ref.shape, BF16)

    for step in range(n_sub + 2):
        if step < n_sub:
            stage_scores(step)
        if 0 <= step - 1 < n_sub:
            stage_softmax(step - 1)
        if 0 <= step - 2 < n_sub:
            stage_pv(step - 2)


def _neighborhood_attention(qkv, batch, rpb):
    t = qkv.shape[1]
    s_len = t // batch
    rows = s_len // GRID_W
    assert rows % NA_SUB_ROWS == 0 and rows >= NA_WIN_ROWS + 2 * NA_SUB_ROWS and NA_KR % 2 == 0
    tables = _na_bias_tables(rpb, rows)
    hp = NA_HEADS // 2
    slab = lambda part: pl.BlockSpec((None, s_len, LANES), lambda pp, bb: (part * hp + pp, bb, 0))
    return pl.pallas_call(
        functools.partial(_na_kernel, rows),
        grid=(hp, batch),
        in_specs=[slab(0), slab(1), slab(2),
                  pl.BlockSpec((1,) + tables.shape[1:], lambda pp, bb: (pp, 0, 0, 0))],
        out_specs=slab(0),
        out_shape=jax.ShapeDtypeStruct((hp, t, LANES), BF16),
        scratch_shapes=[
            pltpu.VMEM((NA_STAGES,) + tables.shape[2:], F32),
            pltpu.VMEM((NA_STAGES,) + tables.shape[2:], BF16),
            pltpu.VMEM((s_len, 2 * LANES), BF16),
        ],
        compiler_params=pltpu.CompilerParams(
            dimension_semantics=("parallel", "arbitrary"), vmem_limit_bytes=VMEM_LIMIT),
        name="neighborhood_attention",
    )(qkv, qkv, qkv, tables)


def _post_kernel(final, o_ref, h_ref, p_ref, wo_ref, gf_ref, wg_ref, wu_ref, wd_ref,
                 gp_ref, wpg_ref, wpp_ref, gfin_ref, out_ref):
    o = jnp.concatenate([o_ref[c] for c in range(N_CB)], axis=-1)
    h = h_ref[...] + jnp.dot(o, wo_ref[...], preferred_element_type=F32)

    n = _rms(h, gf_ref[...], NORM_EPS).astype(BF16)
    acc = h
    for c in range(FFN_HIDDEN // FFN_TF):
        cols = slice(c * FFN_TF, (c + 1) * FFN_TF)
        gate = jnp.dot(n, wg_ref[:, cols], preferred_element_type=F32)
        up = jnp.dot(n, wu_ref[:, cols], preferred_element_type=F32)
        act = (gate * _sigmoid(gate) * up).astype(BF16)
        acc = acc + jnp.dot(act, wd_ref[cols, :], preferred_element_type=F32)
    h = acc

    n = _rms(h, gp_ref[...], NORM_EPS).astype(BF16)
    gate = _sigmoid(jnp.dot(n, wpg_ref[...], preferred_element_type=F32))
    proj = jnp.dot(p_ref[...].astype(BF16), wpp_ref[...], preferred_element_type=F32)
    h = h + gate * proj
    if final:
        h = _rms(h, gfin_ref[...], NORM_EPS)
    out_ref[...] = h


def _post_block(o, h, p, layer, wo, g_ffn, wg, wu, wd, g_ple, wpg, wpp, g_final, final):
    t, d = h.shape
    tok = lambda i: (i, 0)
    const = lambda i: (0, 0)
    resident = lambda a: pl.BlockSpec(a.shape, const, pipeline_mode=pl.Buffered(1))
    return pl.pallas_call(
        functools.partial(_post_kernel, final),
        grid=(t // POST_TM,),
        in_specs=[
            pl.BlockSpec((N_CB, POST_TM, LANES), lambda i: (0, i, 0)),
            pl.BlockSpec((POST_TM, d), tok),
            pl.BlockSpec((None, POST_TM, PLE_DIM), lambda i: (layer, i, 0)),
            resident(wo), resident(g_ffn), resident(wg), resident(wu), resident(wd),
            resident(g_ple), resident(wpg), resident(wpp), resident(g_final),
        ],
        out_specs=pl.BlockSpec((POST_TM, d), tok),
        out_shape=jax.ShapeDtypeStruct((t, d), F32),
        input_output_aliases={1: 0} if layer > 0 else {},
        compiler_params=pltpu.CompilerParams(
            dimension_semantics=("parallel",), vmem_limit_bytes=VMEM_LIMIT),
        name="post_block",
    )(o, h, p, wo, g_ffn, wg, wu, wd, g_ple, wpg, wpp, g_final)


def kernel(x, p, norm_mix, norm_ffn, norm_ple, norm_final, da_w_qkv, da_w_o, da_lambda_q1, da_lambda_k1, da_lambda_q2, da_lambda_k2, da_subln, na_w_qkv, na_b_qkv, na_rpb, na_w_o, ffn_w_gate, ffn_w_up, ffn_w_down, ple_w_proj, ple_w_gate):
    b, s_len, d = x.shape
    t = b * s_len
    row = lambda a: a.reshape(1, -1).astype(F32)
    q_scale = jnp.concatenate([jnp.full((d,), DA_HEAD_DIM ** -0.5, F32), jnp.ones((2 * d,), F32)])
    q_scale = q_scale.reshape(1, -1)
    p = p.reshape(DEPTH, t, PLE_DIM)
    h = x.reshape(t, d)
    for i in range(DEPTH):
        j = i // 2
        if i % 2 == 0:
            qkv = _qkv_proj(h, row(norm_mix[i]), da_w_qkv[j].astype(BF16),
                            jnp.zeros((1, 3 * d), F32), q_scale)
            mix = _diff_attention(qkv, b, da_lambda_q1[j], da_lambda_k1[j],
                                  da_lambda_q2[j], da_lambda_k2[j], da_subln[j], i)
            w_o = da_w_o[j]
        else:
            qkv = _qkv_proj(h, row(norm_mix[i]), na_w_qkv[j].astype(BF16),
                            row(na_b_qkv[j]), q_scale)
            mix = _neighborhood_attention(qkv, b, na_rpb[j])
            w_o = na_w_o[j]
        h = _post_block(mix, h, p, i, w_o.astype(BF16), row(norm_ffn[i]),
                        ffn_w_gate[i].astype(BF16), ffn_w_up[i].astype(BF16),
                        ffn_w_down[i].astype(BF16), row(norm_ple[i]),
                        ple_w_gate[i].astype(BF16), ple_w_proj[i].astype(BF16),
                        row(norm_final), final=(i == DEPTH - 1))
    return h.reshape(b, s_len, d)
```

```python
import functools
import math

import numpy as np
import jax
import jax.numpy as jnp
from jax import lax
from jax.experimental import pallas as pl
from jax.experimental.pallas import tpu as pltpu

F32 = jnp.float32
BF16 = jnp.bfloat16

D_MODEL = 1024
DEPTH = 4
PLE_DIM = 256
FFN_HIDDEN = 2816
DA_HEADS = 8
DA_HEAD_DIM = 64
NA_HEADS = 16
NA_HEAD_DIM = 64
GRID_W = 64
NA_KR = 8
NA_KC = 16
NORM_EPS = 1e-6
SUBLN_EPS = 1e-5

LANES = 128
MXU_WIDTH = 256
VMEM_LIMIT = 56 * 1024 * 1024
N_CB = D_MODEL // LANES

QKV_TM = 1024
QKV_TN = 1024
POST_TM = 512
FFN_TF = MXU_WIDTH
DA_TQ = MXU_WIDTH
DA_RC = 32
DA_TILES = 8
POS_SHIFT = 6
assert (1 << POS_SHIFT) == GRID_W == DA_HEAD_DIM
NA_SUB_ROWS = 2
NA_WIN_ROWS = NA_KR + 2
NA_STAGES = 3


def _rms(x, g, eps):
    return x * lax.rsqrt(jnp.mean(x * x, axis=-1, keepdims=True) + eps) * g


def _sigmoid(x):
    return 1.0 / (1.0 + jnp.exp(-x))


def _dot_nt(a, b):
    return lax.dot_general(a, b, (((1,), (1,)), ((), ())), preferred_element_type=F32)


def _qkv_kernel(x_ref, g_ref, w_ref, b_ref, cs_ref, o_ref, xn_ref):
    @pl.when(pl.program_id(1) == 0)
    def _():
        xn_ref[...] = _rms(x_ref[...], g_ref[...], NORM_EPS).astype(BF16)

    acc = jnp.dot(xn_ref[...], w_ref[...], preferred_element_type=F32)
    res = ((acc + b_ref[...]) * cs_ref[...]).astype(o_ref.dtype)
    for c in range(QKV_TN // LANES):
        o_ref[c] = res[:, c * LANES:(c + 1) * LANES]


def _qkv_proj(h, g, w, b, cs):
    t, d = h.shape
    n = w.shape[1]
    cb = QKV_TN // LANES
    return pl.pallas_call(
        _qkv_kernel,
        grid=(t // QKV_TM, n // QKV_TN),
        in_specs=[
            pl.BlockSpec((QKV_TM, d), lambda i, j: (i, 0)),
            pl.BlockSpec((1, d), lambda i, j: (0, 0)),
            pl.BlockSpec((d, QKV_TN), lambda i, j: (0, j)),
            pl.BlockSpec((1, QKV_TN), lambda i, j: (0, j)),
            pl.BlockSpec((1, QKV_TN), lambda i, j: (0, j)),
        ],
        out_specs=pl.BlockSpec((cb, QKV_TM, LANES), lambda i, j: (j, i, 0)),
        out_shape=jax.ShapeDtypeStruct((n // LANES, t, LANES), BF16),
        scratch_shapes=[pltpu.VMEM((QKV_TM, d), BF16)],
        compiler_params=pltpu.CompilerParams(
            dimension_semantics=("parallel", "arbitrary"), vmem_limit_bytes=VMEM_LIMIT),
        name="qkv_proj",
    )(h, g, w, b, cs)


def _da_key_features(s_len):
    pos = np.arange(s_len)
    f = np.zeros((s_len, LANES), np.float32)
    for base in (0, DA_HEAD_DIM):
        f[:, base + 0] = pos // GRID_W
        f[:, base + 1] = pos % GRID_W
        f[:, base + 2] = 1.0
        f[:, base + 3] = 1.0
    return jnp.asarray(f, BF16)


def _da_diag_correction():
    r = np.arange(DA_TQ)
    return jnp.asarray(-2.0 * np.maximum(r[None, :] - r[:, None], 0), F32)


def _da_kernel(lam_init, nb, slopes_ref, qc_ref, qn_ref, k_ref, v_ref, kfeat_ref, corr_ref,
               lq1_ref, lk1_ref, lq2_ref, lk2_ref, sub_ref, o_ref,
               kaug_ref, vaug_ref, qv_ref, sa_ref, sb_ref, pa_ref, pb_ref):
    h = pl.program_id(1)
    t = pl.program_id(2)
    tq = DA_TQ
    slope = slopes_ref[h]
    corr = slope * corr_ref[...]

    def scores(q, i_tile, slot, s_ref):
        lane = lax.broadcasted_iota(jnp.int32, q.shape, 1)
        fidx = jnp.bitwise_and(lane, DA_HEAD_DIM - 1)
        pos = i_tile * tq + lax.broadcasted_iota(jnp.int32, q.shape, 0)
        pos_hi = jnp.right_shift(pos, POS_SHIFT).astype(F32)
        pos_lo = jnp.bitwise_and(pos, GRID_W - 1).astype(F32)
        feat = jnp.where(fidx == 0, GRID_W * slope,
                         jnp.where(fidx == 1, slope,
                                   jnp.where(fidx == 2, -GRID_W * slope * pos_hi,
                                             jnp.where(fidx == 3, -slope * pos_lo, 0.0))))
        feat_l = feat.astype(BF16)
        feat_r = (-feat).astype(BF16)
        for m in range(2):
            is_data = (lane < DA_HEAD_DIM) if m == 0 else (lane >= DA_HEAD_DIM)
            qv_ref[slot, m, 0] = jnp.where(is_data, q, feat_l)
            qv_ref[slot, m, 1] = jnp.where(is_data, q, feat_r)

        def block(m, j):
            side = jnp.where(j > i_tile, 1, 0)
            s_ref[m, j] = _dot_nt(qv_ref[slot, m, side], kaug_ref[m, j * tq:(j + 1) * tq, :])
            if j == nb - 1:
                s_ref[m, i_tile] = s_ref[m, i_tile] + corr

        return [functools.partial(block, m, j) for m in range(2) for j in range(nb)]

    def softmax_chunk(s_ref, p_ref, m, rc):
        rows = slice(rc * DA_RC, (rc + 1) * DA_RC)
        m_el = s_ref[m, 0, rows, :]
        for j in range(1, nb):
            m_el = jnp.maximum(m_el, s_ref[m, j, rows, :])
        mb = jnp.broadcast_to(jnp.max(m_el, axis=-1, keepdims=True), m_el.shape)
        for j in range(nb):
            p_ref[m, rows, j * tq:(j + 1) * tq] = jnp.exp(s_ref[m, j, rows, :] - mb).astype(BF16)

    def softmax_pv(s_ref, p_ref, mxu_tasks):
        n_chunks = tq // DA_RC
        per_chunk = len(mxu_tasks) // (2 * n_chunks)
        outs = []
        for m in range(2):
            for rc in range(n_chunks):
                softmax_chunk(s_ref, p_ref, m, rc)
                for _ in range(per_chunk):
                    mxu_tasks.pop(0)()
            res = jnp.dot(p_ref[m], vaug_ref[...], preferred_element_type=F32)
            outs.append(res[:, :LANES] / res[:, LANES:])
        while mxu_tasks:
            mxu_tasks.pop(0)()
        lam = (jnp.exp(jnp.sum(lq1_ref[...] * lk1_ref[...], axis=-1, keepdims=True))
               - jnp.exp(jnp.sum(lq2_ref[...] * lk2_ref[...], axis=-1, keepdims=True))
               + lam_init)
        o = outs[0] - lam * outs[1]
        o = _rms(o, sub_ref[...], SUBLN_EPS) * (1.0 - lam_init)
        return o.astype(o_ref.dtype)

    @pl.when(t == 0)
    def _():
        k = k_ref[...]
        kf = kfeat_ref[...]
        lane_k = lax.broadcasted_iota(jnp.int32, k.shape, 1)
        kaug_ref[0] = jnp.where(lane_k < DA_HEAD_DIM, k, kf)
        kaug_ref[1] = jnp.where(lane_k >= DA_HEAD_DIM, k, kf)
        vaug_ref[:, :LANES] = v_ref[...]
        vaug_ref[:, LANES:] = jnp.ones(v_ref.shape, BF16)
        for task in scores(qc_ref[:tq, :], DA_TILES * t, 0, sa_ref):
            task()

    s_bufs, p_bufs = (sa_ref, sb_ref), (pa_ref, pb_ref)
    for n in range(DA_TILES):
        if n + 1 < DA_TILES:
            q_next, i_next = qc_ref[(n + 1) * tq:(n + 2) * tq, :], DA_TILES * t + n + 1
        else:
            q_next, i_next = qn_ref[...], jnp.minimum(DA_TILES * (t + 1), nb - 1)
        tasks = scores(q_next, i_next, n + 1, s_bufs[(n + 1) % 2])
        o_ref[n * tq:(n + 1) * tq, :] = softmax_pv(s_bufs[n % 2], p_bufs[n % 2], tasks)


def _diff_attention(qkv, batch, lq1, lk1, lq2, lk2, subln, layer_idx):
    t = qkv.shape[1]
    s_len = t // batch
    nb = s_len // DA_TQ
    lam_init = 0.8 - 0.6 * math.exp(-0.3 * layer_idx)
    slopes = [2.0 ** (-8.0 * (hh + 1) / DA_HEADS) for hh in range(DA_HEADS)]
    for sl in slopes:
        assert float(np.asarray(sl * GRID_W * (GRID_W - 1), dtype=BF16)) == sl * GRID_W * (GRID_W - 1)
    assert s_len <= GRID_W * GRID_W and nb % DA_TILES == 0 and DA_TILES % 2 == 0
    steps = nb // DA_TILES
    step_rows = DA_TILES * DA_TQ
    vec = lambda a: a.reshape(1, -1).astype(F32)
    small = pl.BlockSpec((1, DA_HEAD_DIM), lambda bb, hh, ii: (0, 0))
    return pl.pallas_call(
        functools.partial(_da_kernel, lam_init, nb),
        grid=(batch, DA_HEADS, steps),
        in_specs=[
            pl.BlockSpec(memory_space=pltpu.SMEM),
            pl.BlockSpec((None, step_rows, LANES), lambda bb, hh, ii: (hh, bb * steps + ii, 0)),
            pl.BlockSpec((None, DA_TQ, LANES),
                         lambda bb, hh, ii: (hh, bb * nb + jnp.minimum(DA_TILES * (ii + 1), nb - 1), 0)),
            pl.BlockSpec((None, s_len, LANES), lambda bb, hh, ii: (DA_HEADS + hh, bb, 0)),
            pl.BlockSpec((None, s_len, LANES), lambda bb, hh, ii: (2 * DA_HEADS + hh, bb, 0)),
            pl.BlockSpec((s_len, LANES), lambda bb, hh, ii: (0, 0)),
            pl.BlockSpec((DA_TQ, DA_TQ), lambda bb, hh, ii: (0, 0)),
            small, small, small, small,
            pl.BlockSpec((1, LANES), lambda bb, hh, ii: (0, 0)),
        ],
        out_specs=pl.BlockSpec((None, step_rows, LANES), lambda bb, hh, ii: (hh, bb * steps + ii, 0)),
        out_shape=jax.ShapeDtypeStruct((DA_HEADS, t, LANES), BF16),
        scratch_shapes=[
            pltpu.VMEM((2, s_len, LANES), BF16),
            pltpu.VMEM((s_len, 2 * LANES), BF16),
            pltpu.VMEM((DA_TILES + 1, 2, 2, DA_TQ, LANES), BF16),
            pltpu.VMEM((2, nb, DA_TQ, DA_TQ), F32),
            pltpu.VMEM((2, nb, DA_TQ, DA_TQ), F32),
            pltpu.VMEM((2, DA_TQ, s_len), BF16),
            pltpu.VMEM((2, DA_TQ, s_len), BF16),
        ],
        compiler_params=pltpu.CompilerParams(
            dimension_semantics=("parallel", "parallel", "arbitrary"), vmem_limit_bytes=VMEM_LIMIT),
        name="diff_attention",
    )(jnp.asarray(slopes, F32), qkv, qkv, qkv, qkv, _da_key_features(s_len), _da_diag_correction(),
      vec(lq1), vec(lk1), vec(lq2), vec(lk2), vec(subln))


def _na_sub_geometry(rows):
    last = rows - NA_WIN_ROWS
    return ((0, 0), (NA_SUB_ROWS, 0), (2 * NA_SUB_ROWS, 0),
            (rows - 2 * NA_SUB_ROWS, last), (rows - NA_SUB_ROWS, last))


def _na_bias_tables(rpb, rows):
    w = GRID_W
    kr = min(NA_KR, rows)
    n_dr = 2 * NA_KR - 1
    cq, ck = np.arange(w)[:, None], np.arange(w)[None, :]
    cstart = np.clip(cq - NA_KC // 2, 0, w - NA_KC)
    col_ok = (ck >= cstart) & (ck < cstart + NA_KC)
    col_off = np.clip(ck - cq, -(NA_KC - 1), NA_KC - 1) + (NA_KC - 1)
    onehot = (col_off[..., None] == np.arange(2 * NA_KC - 1)).astype(np.float32)
    tiles = jnp.einsum("hdo,qko->hdqk", rpb.astype(F32), onehot, precision=lax.Precision.HIGHEST)
    tiles = jnp.where(col_ok, tiles, -jnp.inf)
    masked = jnp.full((NA_HEADS, 1, w, w), -jnp.inf, F32)
    tiles = jnp.concatenate([tiles, masked], axis=1)
    tables = []
    for rq0, ws in _na_sub_geometry(rows):
        block_rows = []
        for qi in range(NA_SUB_ROWS):
            rq = rq0 + qi
            rs = min(max(rq - kr // 2, 0), rows - kr)
            picks = []
            for kj in range(NA_WIN_ROWS):
                rk = ws + kj
                picks.append(rk - rq + (NA_KR - 1) if rs <= rk < rs + kr else n_dr)
            block_rows.append(jnp.concatenate([tiles[:, dr] for dr in picks], axis=-1))
        tables.append(jnp.concatenate(block_rows, axis=1))
    t = jnp.stack(tables, axis=1)
    nq, nk = t.shape[2], t.shape[3]
    t = t.reshape(NA_HEADS // 2, 2, 5, nq, nk).transpose(0, 2, 1, 3, 4)
    return t.reshape(NA_HEADS // 2, 5, 2 * nq, nk)


def _na_kernel(rows, q_ref, k_ref, v_ref, tab_ref, o_ref, s_ref, p_ref, vaug_ref):
    n_sub = rows // NA_SUB_ROWS
    nq = NA_SUB_ROWS * GRID_W
    nk = NA_WIN_ROWS * GRID_W
    lane = lax.broadcasted_iota(jnp.int32, (nq, LANES), 1)

    def geometry(g):
        ws = min(max(NA_SUB_ROWS * g - NA_KR // 2, 0), rows - NA_WIN_ROWS)
        kind = g if g < 2 else (g - (n_sub - 5) if g >= n_sub - 2 else 2)
        return g * nq, ws * GRID_W, kind

    def stage_scores(g):
        q0, k0, kind = geometry(g)
        qg = q_ref[q0:q0 + nq, :]
        zero = jnp.zeros_like(qg)
        q2 = jnp.concatenate([jnp.where(lane < NA_HEAD_DIM, qg, zero),
                              jnp.where(lane >= NA_HEAD_DIM, qg, zero)], axis=0)
        s_ref[g % NA_STAGES] = _dot_nt(q2, k_ref[k0:k0 + nk, :]) + tab_ref[0, kind]

    def stage_softmax(g):
        s = s_ref[g % NA_STAGES]
        p_ref[g % NA_STAGES] = jnp.exp(s - jnp.max(s, axis=-1, keepdims=True)).astype(BF16)

    def stage_pv(g):
        q0, k0, _ = geometry(g)
        res = jnp.dot(p_ref[g % NA_STAGES], vaug_ref[k0:k0 + nk, :], preferred_element_type=F32)
        pv = res[:, :LANES] / res[:, LANES:]
        o_ref[q0:q0 + nq, :] = jnp.where(lane < NA_HEAD_DIM, pv[:nq], pv[nq:]).astype(o_ref.dtype)

    vaug_ref[:, :LANES] = v_ref[...]
    vaug_ref[:, LANES:] = jnp.ones(v_ref.shape, BF16)

    for step in range(n_sub + 2):
        if step < n_sub:
            stage_scores(step)
        if 0 <= step - 1 < n_sub:
            stage_softmax(step - 1)
        if 0 <= step - 2 < n_sub:
            stage_pv(step - 2)


def _neighborhood_attention(qkv, batch, rpb):
    t = qkv.shape[1]
    s_len = t // batch
    rows = s_len // GRID_W
    assert rows % NA_SUB_ROWS == 0 and rows >= NA_WIN_ROWS + 2 * NA_SUB_ROWS and NA_KR % 2 == 0
    tables = _na_bias_tables(rpb, rows)
    hp = NA_HEADS // 2
    slab = lambda part: pl.BlockSpec((None, s_len, LANES), lambda pp, bb: (part * hp + pp, bb, 0))
    return pl.pallas_call(
        functools.partial(_na_kernel, rows),
        grid=(hp, batch),
        in_specs=[slab(0), slab(1), slab(2),
                  pl.BlockSpec((1,) + tables.shape[1:], lambda pp, bb: (pp, 0, 0, 0))],
        out_specs=slab(0),
        out_shape=jax.ShapeDtypeStruct((hp, t, LANES), BF16),
        scratch_shapes=[
            pltpu.VMEM((NA_STAGES,) + tables.shape[2:], F32),
            pltpu.VMEM((NA_STAGES,) + tables.shape[2:], BF16),
            pltpu.VMEM((s_len, 2 * LANES), BF16),
        ],
        compiler_params=pltpu.CompilerParams(
            dimension_semantics=("parallel", "arbitrary"), vmem_limit_bytes=VMEM_LIMIT),
        name="neighborhood_attention",
    )(qkv, qkv, qkv, tables)


def _post_kernel(final, o_ref, h_ref, p_ref, wo_ref, gf_ref, wg_ref, wu_ref, wd_ref,
                 gp_ref, wpg_ref, wpp_ref, gfin_ref, out_ref):
    o = jnp.concatenate([o_ref[c] for c in range(N_CB)], axis=-1)
    h = h_ref[...] + jnp.dot(o, wo_ref[...], preferred_element_type=F32)

    n = _rms(h, gf_ref[...], NORM_EPS).astype(BF16)
    acc = h
    for c in range(FFN_HIDDEN // FFN_TF):
        cols = slice(c * FFN_TF, (c + 1) * FFN_TF)
        gate = jnp.dot(n, wg_ref[:, cols], preferred_element_type=F32)
        up = jnp.dot(n, wu_ref[:, cols], preferred_element_type=F32)
        act = (gate * _sigmoid(gate) * up).astype(BF16)
        acc = acc + jnp.dot(act, wd_ref[cols, :], preferred_element_type=F32)
    h = acc

    n = _rms(h, gp_ref[...], NORM_EPS).astype(BF16)
    gate = _sigmoid(jnp.dot(n, wpg_ref[...], preferred_element_type=F32))
    proj = jnp.dot(p_ref[...].astype(BF16), wpp_ref[...], preferred_element_type=F32)
    h = h + gate * proj
    if final:
        h = _rms(h, gfin_ref[...], NORM_EPS)
    out_ref[...] = h


def _post_block(o, h, p, layer, wo, g_ffn, wg, wu, wd, g_ple, wpg, wpp, g_final, final):
    t, d = h.shape
    tok = lambda i: (i, 0)
    const = lambda i: (0, 0)
    resident = lambda a: pl.BlockSpec(a.shape, const, pipeline_mode=pl.Buffered(1))
    return pl.pallas_call(
        functools.partial(_post_kernel, final),
        grid=(t // POST_TM,),
        in_specs=[
            pl.BlockSpec((N_CB, POST_TM, LANES), lambda i: (0, i, 0)),
            pl.BlockSpec((POST_TM, d), tok),
            pl.BlockSpec((None, POST_TM, PLE_DIM), lambda i: (layer, i, 0)),
            resident(wo), resident(g_ffn), resident(wg), resident(wu), resident(wd),
            resident(g_ple), resident(wpg), resident(wpp), resident(g_final),
        ],
        out_specs=pl.BlockSpec((POST_TM, d), tok),
        out_shape=jax.ShapeDtypeStruct((t, d), F32),
        input_output_aliases={1: 0} if layer > 0 else {},
        compiler_params=pltpu.CompilerParams(
            dimension_semantics=("parallel",), vmem_limit_bytes=VMEM_LIMIT),
        name="post_block",
    )(o, h, p, wo, g_ffn, wg, wu, wd, g_ple, wpg, wpp, g_final)


def kernel(x, p, norm_mix, norm_ffn, norm_ple, norm_final, da_w_qkv, da_w_o, da_lambda_q1, da_lambda_k1, da_lambda_q2, da_lambda_k2, da_subln, na_w_qkv, na_b_qkv, na_rpb, na_w_o, ffn_w_gate, ffn_w_up, ffn_w_down, ple_w_proj, ple_w_gate):
    b, s_len, d = x.shape
    t = b * s_len
    row = lambda a: a.reshape(1, -1).astype(F32)
    q_scale = jnp.concatenate([jnp.full((d,), DA_HEAD_DIM ** -0.5, F32), jnp.ones((2 * d,), F32)])
    q_scale = q_scale.reshape(1, -1)
    p = p.reshape(DEPTH, t, PLE_DIM)
    h = x.reshape(t, d)
    for i in range(DEPTH):
        j = i // 2
        if i % 2 == 0:
            qkv = _qkv_proj(h, row(norm_mix[i]), da_w_qkv[j].astype(BF16),
                            jnp.zeros((1, 3 * d), F32), q_scale)
            mix = _diff_attention(qkv, b, da_lambda_q1[j], da_lambda_k1[j],
                                  da_lambda_q2[j], da_lambda_k2[j], da_subln[j], i)
            w_o = da_w_o[j]
        else:
            qkv = _qkv_proj(h, row(norm_mix[i]), na_w_qkv[j].astype(BF16),
                            row(na_b_qkv[j]), q_scale)
            mix = _neighborhood_attention(qkv, b, na_rpb[j])
            w_o = na_w_o[j]
        h = _post_block(mix, h, p, i, w_o.astype(BF16), row(norm_ffn[i]),
                        ffn_w_gate[i].astype(BF16), ffn_w_up[i].astype(BF16),
                        ffn_w_down[i].astype(BF16), row(norm_ple[i]),
                        ple_w_gate[i].astype(BF16), ple_w_proj[i].astype(BF16),
                        row(norm_final), final=(i == DEPTH - 1))
    return h.reshape(b, s_len, d)
```

```python
import functools
import math

import numpy as np
import jax
import jax.numpy as jnp
from jax import lax
from jax.experimental import pallas as pl
from jax.experimental.pallas import tpu as pltpu

F32 = jnp.float32
BF16 = jnp.bfloat16

D_MODEL = 1024
DEPTH = 4
PLE_DIM = 256
FFN_HIDDEN = 2816
DA_HEADS = 8
DA_HEAD_DIM = 64
NA_HEADS = 16
NA_HEAD_DIM = 64
GRID_W = 64
NA_KR = 8
NA_KC = 16
NORM_EPS = 1e-6
SUBLN_EPS = 1e-5

LANES = 128
MXU_WIDTH = 256
VMEM_LIMIT = 56 * 1024 * 1024
N_CB = D_MODEL // LANES

QKV_TM = 1024
QKV_TN = 1024
POST_TM = 512
FFN_TF = MXU_WIDTH
DA_TQ = MXU_WIDTH
DA_RC = 32
DA_TILES = 16
POS_SHIFT = 6
assert (1 << POS_SHIFT) == GRID_W == DA_HEAD_DIM
NA_SUB_ROWS = 2
NA_WIN_ROWS = NA_KR + 2
NA_STAGES = 3


def _rms(x, g, eps):
    return x * lax.rsqrt(jnp.mean(x * x, axis=-1, keepdims=True) + eps) * g


def _sigmoid(x):
    return 1.0 / (1.0 + jnp.exp(-x))


def _dot_nt(a, b):
    return lax.dot_general(a, b, (((1,), (1,)), ((), ())), preferred_element_type=F32)


def _qkv_kernel(x_ref, g_ref, w_ref, b_ref, cs_ref, o_ref, xn_ref):
    @pl.when(pl.program_id(1) == 0)
    def _():
        xn_ref[...] = _rms(x_ref[...], g_ref[...], NORM_EPS).astype(BF16)

    acc = jnp.dot(xn_ref[...], w_ref[...], preferred_element_type=F32)
    res = ((acc + b_ref[...]) * cs_ref[...]).astype(o_ref.dtype)
    for c in range(QKV_TN // LANES):
        o_ref[c] = res[:, c * LANES:(c + 1) * LANES]


def _qkv_proj(h, g, w, b, cs):
    t, d = h.shape
    n = w.shape[1]
    cb = QKV_TN // LANES
    return pl.pallas_call(
        _qkv_kernel,
        grid=(t // QKV_TM, n // QKV_TN),
        in_specs=[
            pl.BlockSpec((QKV_TM, d), lambda i, j: (i, 0)),
            pl.BlockSpec((1, d), lambda i, j: (0, 0)),
            pl.BlockSpec((d, QKV_TN), lambda i, j: (0, j)),
            pl.BlockSpec((1, QKV_TN), lambda i, j: (0, j)),
            pl.BlockSpec((1, QKV_TN), lambda i, j: (0, j)),
        ],
        out_specs=pl.BlockSpec((cb, QKV_TM, LANES), lambda i, j: (j, i, 0)),
        out_shape=jax.ShapeDtypeStruct((n // LANES, t, LANES), BF16),
        scratch_shapes=[pltpu.VMEM((QKV_TM, d), BF16)],
        compiler_params=pltpu.CompilerParams(
            dimension_semantics=("parallel", "arbitrary"), vmem_limit_bytes=VMEM_LIMIT),
        name="qkv_proj",
    )(h, g, w, b, cs)


def _da_key_features(s_len):
    pos = np.arange(s_len)
    f = np.zeros((s_len, LANES), np.float32)
    for base in (0, DA_HEAD_DIM):
        f[:, base + 0] = pos // GRID_W
        f[:, base + 1] = pos % GRID_W
        f[:, base + 2] = 1.0
        f[:, base + 3] = 1.0
    return jnp.asarray(f, BF16)


def _da_diag_correction():
    r = np.arange(DA_TQ)
    return jnp.asarray(-2.0 * np.maximum(r[None, :] - r[:, None], 0), F32)


def _da_kernel(lam_init, nb, slopes_ref, qc_ref, qn_ref, k_ref, v_ref, kfeat_ref, corr_ref,
               lq1_ref, lk1_ref, lq2_ref, lk2_ref, sub_ref, o_ref,
               kaug_ref, vaug_ref, qv_ref, sa_ref, sb_ref, pa_ref, pb_ref):
    h = pl.program_id(1)
    t = pl.program_id(2)
    tq = DA_TQ
    slope = slopes_ref[h]
    corr = slope * corr_ref[...]

    def scores(q, i_tile, slot, s_ref):
        lane = lax.broadcasted_iota(jnp.int32, q.shape, 1)
        fidx = jnp.bitwise_and(lane, DA_HEAD_DIM - 1)
        pos = i_tile * tq + lax.broadcasted_iota(jnp.int32, q.shape, 0)
        pos_hi = jnp.right_shift(pos, POS_SHIFT).astype(F32)
        pos_lo = jnp.bitwise_and(pos, GRID_W - 1).astype(F32)
        feat = jnp.where(fidx == 0, GRID_W * slope,
                         jnp.where(fidx == 1, slope,
                                   jnp.where(fidx == 2, -GRID_W * slope * pos_hi,
                                             jnp.where(fidx == 3, -slope * pos_lo, 0.0))))
        feat_l = feat.astype(BF16)
        feat_r = (-feat).astype(BF16)
        for m in range(2):
            is_data = (lane < DA_HEAD_DIM) if m == 0 else (lane >= DA_HEAD_DIM)
            qv_ref[slot, m, 0] = jnp.where(is_data, q, feat_l)
            qv_ref[slot, m, 1] = jnp.where(is_data, q, feat_r)

        def block(m, j):
            side = jnp.where(j > i_tile, 1, 0)
            s_ref[m, j] = _dot_nt(qv_ref[slot, m, side], kaug_ref[m, j * tq:(j + 1) * tq, :])
            if j == nb - 1:
                s_ref[m, i_tile] = s_ref[m, i_tile] + corr

        return [functools.partial(block, m, j) for m in range(2) for j in range(nb)]

    def softmax_chunk(s_ref, p_ref, m, rc):
        rows = slice(rc * DA_RC, (rc + 1) * DA_RC)
        m_el = s_ref[m, 0, rows, :]
        for j in range(1, nb):
            m_el = jnp.maximum(m_el, s_ref[m, j, rows, :])
        mb = jnp.broadcast_to(jnp.max(m_el, axis=-1, keepdims=True), m_el.shape)
        for j in range(nb):
            p_ref[m, rows, j * tq:(j + 1) * tq] = jnp.exp(s_ref[m, j, rows, :] - mb).astype(BF16)

    def softmax_pv(s_ref, p_ref, mxu_tasks):
        n_chunks = tq // DA_RC
        per_chunk = len(mxu_tasks) // (2 * n_chunks)
        outs = []
        for m in range(2):
            for rc in range(n_chunks):
                softmax_chunk(s_ref, p_ref, m, rc)
                for _ in range(per_chunk):
                    mxu_tasks.pop(0)()
            res = jnp.dot(p_ref[m], vaug_ref[...], preferred_element_type=F32)
            outs.append(res[:, :LANES] / res[:, LANES:])
        while mxu_tasks:
            mxu_tasks.pop(0)()
        lam = (jnp.exp(jnp.sum(lq1_ref[...] * lk1_ref[...], axis=-1, keepdims=True))
               - jnp.exp(jnp.sum(lq2_ref[...] * lk2_ref[...], axis=-1, keepdims=True))
               + lam_init)
        o = outs[0] - lam * outs[1]
        o = _rms(o, sub_ref[...], SUBLN_EPS) * (1.0 - lam_init)
        return o.astype(o_ref.dtype)

    @pl.when(t == 0)
    def _():
        k = k_ref[...]
        kf = kfeat_ref[...]
        lane_k = lax.broadcasted_iota(jnp.int32, k.shape, 1)
        kaug_ref[0] = jnp.where(lane_k < DA_HEAD_DIM, k, kf)
        kaug_ref[1] = jnp.where(lane_k >= DA_HEAD_DIM, k, kf)
        vaug_ref[:, :LANES] = v_ref[...]
        vaug_ref[:, LANES:] = jnp.ones(v_ref.shape, BF16)
        for task in scores(qc_ref[:tq, :], DA_TILES * t, 0, sa_ref):
            task()

    s_bufs, p_bufs = (sa_ref, sb_ref), (pa_ref, pb_ref)
    for n in range(DA_TILES):
        if n + 1 < DA_TILES:
            tasks = scores(qc_ref[(n + 1) * tq:(n + 2) * tq, :], DA_TILES * t + n + 1, n + 1,
                           s_bufs[(n + 1) % 2])
        elif DA_TILES < nb:
            tasks = scores(qn_ref[...], jnp.minimum(DA_TILES * (t + 1), nb - 1), n + 1,
                           s_bufs[(n + 1) % 2])
        else:
            tasks = []
        o_ref[n * tq:(n + 1) * tq, :] = softmax_pv(s_bufs[n % 2], p_bufs[n % 2], tasks)


def _diff_attention(qkv, batch, lq1, lk1, lq2, lk2, subln, layer_idx):
    t = qkv.shape[1]
    s_len = t // batch
    nb = s_len // DA_TQ
    lam_init = 0.8 - 0.6 * math.exp(-0.3 * layer_idx)
    slopes = [2.0 ** (-8.0 * (hh + 1) / DA_HEADS) for hh in range(DA_HEADS)]
    for sl in slopes:
        assert float(np.asarray(sl * GRID_W * (GRID_W - 1), dtype=BF16)) == sl * GRID_W * (GRID_W - 1)
    assert s_len <= GRID_W * GRID_W and nb % DA_TILES == 0 and DA_TILES % 2 == 0
    steps = nb // DA_TILES
    step_rows = DA_TILES * DA_TQ
    vec = lambda a: a.reshape(1, -1).astype(F32)
    small = pl.BlockSpec((1, DA_HEAD_DIM), lambda bb, hh, ii: (0, 0))
    return pl.pallas_call(
        functools.partial(_da_kernel, lam_init, nb),
        grid=(batch, DA_HEADS, steps),
        in_specs=[
            pl.BlockSpec(memory_space=pltpu.SMEM),
            pl.BlockSpec((None, step_rows, LANES), lambda bb, hh, ii: (hh, bb * steps + ii, 0)),
            pl.BlockSpec((None, DA_TQ, LANES),
                         lambda bb, hh, ii: (hh, bb * nb + jnp.minimum(DA_TILES * (ii + 1), nb - 1), 0)),
            pl.BlockSpec((None, s_len, LANES), lambda bb, hh, ii: (DA_HEADS + hh, bb, 0)),
            pl.BlockSpec((None, s_len, LANES), lambda bb, hh, ii: (2 * DA_HEADS + hh, bb, 0)),
            pl.BlockSpec((s_len, LANES), lambda bb, hh, ii: (0, 0)),
            pl.BlockSpec((DA_TQ, DA_TQ), lambda bb, hh, ii: (0, 0)),
            small, small, small, small,
            pl.BlockSpec((1, LANES), lambda bb, hh, ii: (0, 0)),
        ],
        out_specs=pl.BlockSpec((None, step_rows, LANES), lambda bb, hh, ii: (hh, bb * steps + ii, 0)),
        out_shape=jax.ShapeDtypeStruct((DA_HEADS, t, LANES), BF16),
        scratch_shapes=[
            pltpu.VMEM((2, s_len, LANES), BF16),
            pltpu.VMEM((s_len, 2 * LANES), BF16),
            pltpu.VMEM((DA_TILES + 1, 2, 2, DA_TQ, LANES), BF16),
            pltpu.VMEM((2, nb, DA_TQ, DA_TQ), F32),
            pltpu.VMEM((2, nb, DA_TQ, DA_TQ), F32),
            pltpu.VMEM((2, DA_TQ, s_len), BF16),
            pltpu.VMEM((2, DA_TQ, s_len), BF16),
        ],
        compiler_params=pltpu.CompilerParams(
            dimension_semantics=("parallel", "parallel", "arbitrary"), vmem_limit_bytes=VMEM_LIMIT),
        name="diff_attention",
    )(jnp.asarray(slopes, F32), qkv, qkv, qkv, qkv, _da_key_features(s_len), _da_diag_correction(),
      vec(lq1), vec(lk1), vec(lq2), vec(lk2), vec(subln))


def _na_sub_geometry(rows):
    last = rows - NA_WIN_ROWS
    return ((0, 0), (NA_SUB_ROWS, 0), (2 * NA_SUB_ROWS, 0),
            (rows - 2 * NA_SUB_ROWS, last), (rows - NA_SUB_ROWS, last))


def _na_bias_tables(rpb, rows):
    w = GRID_W
    kr = min(NA_KR, rows)
    n_dr = 2 * NA_KR - 1
    cq, ck = np.arange(w)[:, None], np.arange(w)[None, :]
    cstart = np.clip(cq - NA_KC // 2, 0, w - NA_KC)
    col_ok = (ck >= cstart) & (ck < cstart + NA_KC)
    col_off = np.clip(ck - cq, -(NA_KC - 1), NA_KC - 1) + (NA_KC - 1)
    onehot = (col_off[..., None] == np.arange(2 * NA_KC - 1)).astype(np.float32)
    tiles = jnp.einsum("hdo,qko->hdqk", rpb.astype(F32), onehot, precision=lax.Precision.HIGHEST)
    tiles = jnp.where(col_ok, tiles, -jnp.inf)
    masked = jnp.full((NA_HEADS, 1, w, w), -jnp.inf, F32)
    tiles = jnp.concatenate([tiles, masked], axis=1)
    tables = []
    for rq0, ws in _na_sub_geometry(rows):
        block_rows = []
        for qi in range(NA_SUB_ROWS):
            rq = rq0 + qi
            rs = min(max(rq - kr // 2, 0), rows - kr)
            picks = []
            for kj in range(NA_WIN_ROWS):
                rk = ws + kj
                picks.append(rk - rq + (NA_KR - 1) if rs <= rk < rs + kr else n_dr)
            block_rows.append(jnp.concatenate([tiles[:, dr] for dr in picks], axis=-1))
        tables.append(jnp.concatenate(block_rows, axis=1))
    t = jnp.stack(tables, axis=1)
    nq, nk = t.shape[2], t.shape[3]
    t = t.reshape(NA_HEADS // 2, 2, 5, nq, nk).transpose(0, 2, 1, 3, 4)
    return t.reshape(NA_HEADS // 2, 5, 2 * nq, nk)


def _na_kernel(rows, q_ref, k_ref, v_ref, tab_ref, o_ref, s_ref, p_ref, vaug_ref):
    n_sub = rows // NA_SUB_ROWS
    nq = NA_SUB_ROWS * GRID_W
    nk = NA_WIN_ROWS * GRID_W
    lane = lax.broadcasted_iota(jnp.int32, (nq, LANES), 1)

    def geometry(g):
        ws = min(max(NA_SUB_ROWS * g - NA_KR // 2, 0), rows - NA_WIN_ROWS)
        kind = g if g < 2 else (g - (n_sub - 5) if g >= n_sub - 2 else 2)
        return g * nq, ws * GRID_W, kind

    def stage_scores(g):
        q0, k0, kind = geometry(g)
        qg = q_ref[q0:q0 + nq, :]
        zero = jnp.zeros_like(qg)
        q2 = jnp.concatenate([jnp.where(lane < NA_HEAD_DIM, qg, zero),
                              jnp.where(lane >= NA_HEAD_DIM, qg, zero)], axis=0)
        s_ref[g % NA_STAGES] = _dot_nt(q2, k_ref[k0:k0 + nk, :]) + tab_ref[0, kind]

    def stage_softmax(g):
        s = s_ref[g % NA_STAGES]
        p_ref[g % NA_STAGES] = jnp.exp(s - jnp.max(s, axis=-1, keepdims=True)).astype(BF16)

    def stage_pv(g):
        q0, k0, _ = geometry(g)
        res = jnp.dot(p_ref[g % NA_STAGES], vaug_ref[k0:k0 + nk, :], preferred_element_type=F32)
        pv = res[:, :LANES] / res[:, LANES:]
        o_ref[q0:q0 + nq, :] = jnp.where(lane < NA_HEAD_DIM, pv[:nq], pv[nq:]).astype(o_ref.dtype)

    vaug_ref[:, :LANES] = v_ref[...]
    vaug_ref[:, LANES:] = jnp.ones(v_ref.shape, BF16)

    for step in range(n_sub + 2):
        if step < n_sub:
            stage_scores(step)
        if 0 <= step - 1 < n_sub:
            stage_softmax(step - 1)
        if 0 <= step - 2 < n_sub:
            stage_pv(step - 2)


def _neighborhood_attention(qkv, batch, rpb):
    t = qkv.shape[1]
    s_len = t // batch
    rows = s_len // GRID_W
    assert rows % NA_SUB_ROWS == 0 and rows >= NA_WIN_ROWS + 2 * NA_SUB_ROWS and NA_KR % 2 == 0
    tables = _na_bias_tables(rpb, rows)
    hp = NA_HEADS // 2
    slab = lambda part: pl.BlockSpec((None, s_len, LANES), lambda pp, bb: (part * hp + pp, bb, 0))
    return pl.pallas_call(
        functools.partial(_na_kernel, rows),
        grid=(hp, batch),
        in_specs=[slab(0), slab(1), slab(2),
                  pl.BlockSpec((1,) + tables.shape[1:], lambda pp, bb: (pp, 0, 0, 0))],
        out_specs=slab(0),
        out_shape=jax.ShapeDtypeStruct((hp, t, LANES), BF16),
        scratch_shapes=[
            pltpu.VMEM((NA_STAGES,) + tables.shape[2:], F32),
            pltpu.VMEM((NA_STAGES,) + tables.shape[2:], BF16),
            pltpu.VMEM((s_len, 2 * LANES), BF16),
        ],
        compiler_params=pltpu.CompilerParams(
            dimension_semantics=("parallel", "arbitrary"), vmem_limit_bytes=VMEM_LIMIT),
        name="neighborhood_attention",
    )(qkv, qkv, qkv, tables)


def _post_kernel(final, o_ref, h_ref, p_ref, wo_ref, gf_ref, wg_ref, wu_ref, wd_ref,
                 gp_ref, wpg_ref, wpp_ref, gfin_ref, out_ref):
    o = jnp.concatenate([o_ref[c] for c in range(N_CB)], axis=-1)
    h = h_ref[...] + jnp.dot(o, wo_ref[...], preferred_element_type=F32)

    n = _rms(h, gf_ref[...], NORM_EPS).astype(BF16)
    acc = h
    for c in range(FFN_HIDDEN // FFN_TF):
        cols = slice(c * FFN_TF, (c + 1) * FFN_TF)
        gate = jnp.dot(n, wg_ref[:, cols], preferred_element_type=F32)
        up = jnp.dot(n, wu_ref[:, cols], preferred_element_type=F32)
        act = (gate * _sigmoid(gate) * up).astype(BF16)
        acc = acc + jnp.dot(act, wd_ref[cols, :], preferred_element_type=F32)
    h = acc

    n = _rms(h, gp_ref[...], NORM_EPS).astype(BF16)
    gate = _sigmoid(jnp.dot(n, wpg_ref[...], preferred_element_type=F32))
    proj = jnp.dot(p_ref[...].astype(BF16), wpp_ref[...], preferred_element_type=F32)
    h = h + gate * proj
    if final:
        h = _rms(h, gfin_ref[...], NORM_EPS)
    out_ref[...] = h


def _post_block(o, h, p, layer, wo, g_ffn, wg, wu, wd, g_ple, wpg, wpp, g_final, final):
    t, d = h.shape
    tok = lambda i: (i, 0)
    const = lambda i: (0, 0)
    resident = lambda a: pl.BlockSpec(a.shape, const, pipeline_mode=pl.Buffered(1))
    return pl.pallas_call(
        functools.partial(_post_kernel, final),
        grid=(t // POST_TM,),
        in_specs=[
            pl.BlockSpec((N_CB, POST_TM, LANES), lambda i: (0, i, 0)),
            pl.BlockSpec((POST_TM, d), tok),
            pl.BlockSpec((None, POST_TM, PLE_DIM), lambda i: (layer, i, 0)),
            resident(wo), resident(g_ffn), resident(wg), resident(wu), resident(wd),
            resident(g_ple), resident(wpg), resident(wpp), resident(g_final),
        ],
        out_specs=pl.BlockSpec((POST_TM, d), tok),
        out_shape=jax.ShapeDtypeStruct((t, d), F32),
        input_output_aliases={1: 0} if layer > 0 else {},
        compiler_params=pltpu.CompilerParams(
            dimension_semantics=("parallel",), vmem_limit_bytes=VMEM_LIMIT),
        name="post_block",
    )(o, h, p, wo, g_ffn, wg, wu, wd, g_ple, wpg, wpp, g_final)


def kernel(x, p, norm_mix, norm_ffn, norm_ple, norm_final, da_w_qkv, da_w_o, da_lambda_q1, da_lambda_k1, da_lambda_q2, da_lambda_k2, da_subln, na_w_qkv, na_b_qkv, na_rpb, na_w_o, ffn_w_gate, ffn_w_up, ffn_w_down, ple_w_proj, ple_w_gate):
    b, s_len, d = x.shape
    t = b * s_len
    row = lambda a: a.reshape(1, -1).astype(F32)
    q_scale = jnp.concatenate([jnp.full((d,), DA_HEAD_DIM ** -0.5, F32), jnp.ones((2 * d,), F32)])
    q_scale = q_scale.reshape(1, -1)
    p = p.reshape(DEPTH, t, PLE_DIM)
    h = x.reshape(t, d)
    for i in range(DEPTH):
        j = i // 2
        if i % 2 == 0:
            qkv = _qkv_proj(h, row(norm_mix[i]), da_w_qkv[j].astype(BF16),
                            jnp.zeros((1, 3 * d), F32), q_scale)
            mix = _diff_attention(qkv, b, da_lambda_q1[j], da_lambda_k1[j],
                                  da_lambda_q2[j], da_lambda_k2[j], da_subln[j], i)
            w_o = da_w_o[j]
        else:
            qkv = _qkv_proj(h, row(norm_mix[i]), na_w_qkv[j].astype(BF16),
                            row(na_b_qkv[j]), q_scale)
            mix = _neighborhood_attention(qkv, b, na_rpb[j])
            w_o = na_w_o[j]
        h = _post_block(mix, h, p, i, w_o.astype(BF16), row(norm_ffn[i]),
                        ffn_w_gate[i].astype(BF16), ffn_w_up[i].astype(BF16),
                        ffn_w_down[i].astype(BF16), row(norm_ple[i]),
                        ple_w_gate[i].astype(BF16), ple_w_proj[i].astype(BF16),
                        row(norm_final), final=(i == DEPTH - 1))
    return h.reshape(b, s_len, d)
```

```python
import functools
import math

import numpy as np
import jax
import jax.numpy as jnp
from jax import lax
from jax.experimental import pallas as pl
from jax.experimental.pallas import tpu as pltpu

F32 = jnp.float32
BF16 = jnp.bfloat16

D_MODEL = 1024
DEPTH = 4
PLE_DIM = 256
FFN_HIDDEN = 2816
DA_HEADS = 8
DA_HEAD_DIM = 64
NA_HEADS = 16
NA_HEAD_DIM = 64
GRID_W = 64
NA_KR = 8
NA_KC = 16
NORM_EPS = 1e-6
SUBLN_EPS = 1e-5

LANES = 128
MXU_WIDTH = 256
VMEM_LIMIT = 56 * 1024 * 1024
N_CB = D_MODEL // LANES

QKV_TM = 1024
QKV_TN = 1024
POST_TM = 512
FFN_TF = MXU_WIDTH
DA_TQ = MXU_WIDTH
DA_RC = 64
DA_TILES = 8
POS_SHIFT = 6
assert (1 << POS_SHIFT) == GRID_W == DA_HEAD_DIM
NA_SUB_ROWS = 2
NA_WIN_ROWS = NA_KR + 2
NA_STAGES = 3


def _rms(x, g, eps):
    return x * lax.rsqrt(jnp.mean(x * x, axis=-1, keepdims=True) + eps) * g


def _sigmoid(x):
    return 1.0 / (1.0 + jnp.exp(-x))


def _dot_nt(a, b):
    return lax.dot_general(a, b, (((1,), (1,)), ((), ())), preferred_element_type=F32)


def _qkv_kernel(x_ref, g_ref, w_ref, b_ref, cs_ref, o_ref, xn_ref):
    @pl.when(pl.program_id(1) == 0)
    def _():
        xn_ref[...] = _rms(x_ref[...], g_ref[...], NORM_EPS).astype(BF16)

    acc = jnp.dot(xn_ref[...], w_ref[...], preferred_element_type=F32)
    res = ((acc + b_ref[...]) * cs_ref[...]).astype(o_ref.dtype)
    for c in range(QKV_TN // LANES):
        o_ref[c] = res[:, c * LANES:(c + 1) * LANES]


def _qkv_proj(h, g, w, b, cs):
    t, d = h.shape
    n = w.shape[1]
    cb = QKV_TN // LANES
    return pl.pallas_call(
        _qkv_kernel,
        grid=(t // QKV_TM, n // QKV_TN),
        in_specs=[
            pl.BlockSpec((QKV_TM, d), lambda i, j: (i, 0)),
            pl.BlockSpec((1, d), lambda i, j: (0, 0)),
            pl.BlockSpec((d, QKV_TN), lambda i, j: (0, j)),
            pl.BlockSpec((1, QKV_TN), lambda i, j: (0, j)),
            pl.BlockSpec((1, QKV_TN), lambda i, j: (0, j)),
        ],
        out_specs=pl.BlockSpec((cb, QKV_TM, LANES), lambda i, j: (j, i, 0)),
        out_shape=jax.ShapeDtypeStruct((n // LANES, t, LANES), BF16),
        scratch_shapes=[pltpu.VMEM((QKV_TM, d), BF16)],
        compiler_params=pltpu.CompilerParams(
            dimension_semantics=("parallel", "arbitrary"), vmem_limit_bytes=VMEM_LIMIT),
        name="qkv_proj",
    )(h, g, w, b, cs)


def _da_key_features(s_len):
    pos = np.arange(s_len)
    f = np.zeros((s_len, LANES), np.float32)
    for base in (0, DA_HEAD_DIM):
        f[:, base + 0] = pos // GRID_W
        f[:, base + 1] = pos % GRID_W
        f[:, base + 2] = 1.0
        f[:, base + 3] = 1.0
    return jnp.asarray(f, BF16)


def _da_diag_correction():
    r = np.arange(DA_TQ)
    return jnp.asarray(-2.0 * np.maximum(r[None, :] - r[:, None], 0), F32)


def _da_kernel(lam_init, nb, slopes_ref, qc_ref, qn_ref, k_ref, v_ref, kfeat_ref, corr_ref,
               lq1_ref, lk1_ref, lq2_ref, lk2_ref, sub_ref, o_ref,
               kaug_ref, vaug_ref, qv_ref, sa_ref, sb_ref, pa_ref, pb_ref):
    h = pl.program_id(1)
    t = pl.program_id(2)
    tq = DA_TQ
    slope = slopes_ref[h]
    corr = slope * corr_ref[...]

    def scores(q, i_tile, slot, s_ref):
        lane = lax.broadcasted_iota(jnp.int32, q.shape, 1)
        fidx = jnp.bitwise_and(lane, DA_HEAD_DIM - 1)
        pos = i_tile * tq + lax.broadcasted_iota(jnp.int32, q.shape, 0)
        pos_hi = jnp.right_shift(pos, POS_SHIFT).astype(F32)
        pos_lo = jnp.bitwise_and(pos, GRID_W - 1).astype(F32)
        feat = jnp.where(fidx == 0, GRID_W * slope,
                         jnp.where(fidx == 1, slope,
                                   jnp.where(fidx == 2, -GRID_W * slope * pos_hi,
                                             jnp.where(fidx == 3, -slope * pos_lo, 0.0))))
        feat_l = feat.astype(BF16)
        feat_r = (-feat).astype(BF16)
        for m in range(2):
            is_data = (lane < DA_HEAD_DIM) if m == 0 else (lane >= DA_HEAD_DIM)
            qv_ref[slot, m, 0] = jnp.where(is_data, q, feat_l)
            qv_ref[slot, m, 1] = jnp.where(is_data, q, feat_r)

        def block(m, j):
            side = jnp.where(j > i_tile, 1, 0)
            s_ref[m, j] = _dot_nt(qv_ref[slot, m, side], kaug_ref[m, j * tq:(j + 1) * tq, :])
            if j == nb - 1:
                s_ref[m, i_tile] = s_ref[m, i_tile] + corr

        return [functools.partial(block, m, j) for m in range(2) for j in range(nb)]

    def softmax_chunk(s_ref, p_ref, m, rc):
        rows = slice(rc * DA_RC, (rc + 1) * DA_RC)
        m_el = s_ref[m, 0, rows, :]
        for j in range(1, nb):
            m_el = jnp.maximum(m_el, s_ref[m, j, rows, :])
        mb = jnp.broadcast_to(jnp.max(m_el, axis=-1, keepdims=True), m_el.shape)
        for j in range(nb):
            p_ref[m, rows, j * tq:(j + 1) * tq] = jnp.exp(s_ref[m, j, rows, :] - mb).astype(BF16)

    def softmax_pv(s_ref, p_ref, mxu_tasks):
        n_chunks = tq // DA_RC
        per_chunk = len(mxu_tasks) // (2 * n_chunks)
        outs = []
        for m in range(2):
            for rc in range(n_chunks):
                softmax_chunk(s_ref, p_ref, m, rc)
                for _ in range(per_chunk):
                    mxu_tasks.pop(0)()
            res = jnp.dot(p_ref[m], vaug_ref[...], preferred_element_type=F32)
            outs.append(res[:, :LANES] / res[:, LANES:])
        while mxu_tasks:
            mxu_tasks.pop(0)()
        lam = (jnp.exp(jnp.sum(lq1_ref[...] * lk1_ref[...], axis=-1, keepdims=True))
               - jnp.exp(jnp.sum(lq2_ref[...] * lk2_ref[...], axis=-1, keepdims=True))
               + lam_init)
        o = outs[0] - lam * outs[1]
        o = _rms(o, sub_ref[...], SUBLN_EPS) * (1.0 - lam_init)
        return o.astype(o_ref.dtype)

    @pl.when(t == 0)
    def _():
        k = k_ref[...]
        kf = kfeat_ref[...]
        lane_k = lax.broadcasted_iota(jnp.int32, k.shape, 1)
        kaug_ref[0] = jnp.where(lane_k < DA_HEAD_DIM, k, kf)
        kaug_ref[1] = jnp.where(lane_k >= DA_HEAD_DIM, k, kf)
        vaug_ref[:, :LANES] = v_ref[...]
        vaug_ref[:, LANES:] = jnp.ones(v_ref.shape, BF16)
        for task in scores(qc_ref[:tq, :], DA_TILES * t, 0, sa_ref):
            task()

    s_bufs, p_bufs = (sa_ref, sb_ref), (pa_ref, pb_ref)
    for n in range(DA_TILES):
        if n + 1 < DA_TILES:
            q_next, i_next = qc_ref[(n + 1) * tq:(n + 2) * tq, :], DA_TILES * t + n + 1
        else:
            q_next, i_next = qn_ref[...], jnp.minimum(DA_TILES * (t + 1), nb - 1)
        tasks = scores(q_next, i_next, n + 1, s_bufs[(n + 1) % 2])
        o_ref[n * tq:(n + 1) * tq, :] = softmax_pv(s_bufs[n % 2], p_bufs[n % 2], tasks)


def _diff_attention(qkv, batch, lq1, lk1, lq2, lk2, subln, layer_idx):
    t = qkv.shape[1]
    s_len = t // batch
    nb = s_len // DA_TQ
    lam_init = 0.8 - 0.6 * math.exp(-0.3 * layer_idx)
    slopes = [2.0 ** (-8.0 * (hh + 1) / DA_HEADS) for hh in range(DA_HEADS)]
    for sl in slopes:
        assert float(np.asarray(sl * GRID_W * (GRID_W - 1), dtype=BF16)) == sl * GRID_W * (GRID_W - 1)
    assert s_len <= GRID_W * GRID_W and nb % DA_TILES == 0 and DA_TILES % 2 == 0
    steps = nb // DA_TILES
    step_rows = DA_TILES * DA_TQ
    vec = lambda a: a.reshape(1, -1).astype(F32)
    small = pl.BlockSpec((1, DA_HEAD_DIM), lambda bb, hh, ii: (0, 0))
    return pl.pallas_call(
        functools.partial(_da_kernel, lam_init, nb),
        grid=(batch, DA_HEADS, steps),
        in_specs=[
            pl.BlockSpec(memory_space=pltpu.SMEM),
            pl.BlockSpec((None, step_rows, LANES), lambda bb, hh, ii: (hh, bb * steps + ii, 0)),
            pl.BlockSpec((None, DA_TQ, LANES),
                         lambda bb, hh, ii: (hh, bb * nb + jnp.minimum(DA_TILES * (ii + 1), nb - 1), 0)),
            pl.BlockSpec((None, s_len, LANES), lambda bb, hh, ii: (DA_HEADS + hh, bb, 0)),
            pl.BlockSpec((None, s_len, LANES), lambda bb, hh, ii: (2 * DA_HEADS + hh, bb, 0)),
            pl.BlockSpec((s_len, LANES), lambda bb, hh, ii: (0, 0)),
            pl.BlockSpec((DA_TQ, DA_TQ), lambda bb, hh, ii: (0, 0)),
            small, small, small, small,
            pl.BlockSpec((1, LANES), lambda bb, hh, ii: (0, 0)),
        ],
        out_specs=pl.BlockSpec((None, step_rows, LANES), lambda bb, hh, ii: (hh, bb * steps + ii, 0)),
        out_shape=jax.ShapeDtypeStruct((DA_HEADS, t, LANES), BF16),
        scratch_shapes=[
            pltpu.VMEM((2, s_len, LANES), BF16),
            pltpu.VMEM((s_len, 2 * LANES), BF16),
            pltpu.VMEM((DA_TILES + 1, 2, 2, DA_TQ, LANES), BF16),
            pltpu.VMEM((2, nb, DA_TQ, DA_TQ), F32),
            pltpu.VMEM((2, nb, DA_TQ, DA_TQ), F32),
            pltpu.VMEM((2, DA_TQ, s_len), BF16),
            pltpu.VMEM((2, DA_TQ, s_len), BF16),
        ],
        compiler_params=pltpu.CompilerParams(
            dimension_semantics=("parallel", "parallel", "arbitrary"), vmem_limit_bytes=VMEM_LIMIT),
        name="diff_attention",
    )(jnp.asarray(slopes, F32), qkv, qkv, qkv, qkv, _da_key_features(s_len), _da_diag_correction(),
      vec(lq1), vec(lk1), vec(lq2), vec(lk2), vec(subln))


def _na_sub_geometry(rows):
    last = rows - NA_WIN_ROWS
    return ((0, 0), (NA_SUB_ROWS, 0), (2 * NA_SUB_ROWS, 0),
            (rows - 2 * NA_SUB_ROWS, last), (rows - NA_SUB_ROWS, last))


def _na_bias_tables(rpb, rows):
    w = GRID_W
    kr = min(NA_KR, rows)
    n_dr = 2 * NA_KR - 1
    cq, ck = np.arange(w)[:, None], np.arange(w)[None, :]
    cstart = np.clip(cq - NA_KC // 2, 0, w - NA_KC)
    col_ok = (ck >= cstart) & (ck < cstart + NA_KC)
    col_off = np.clip(ck - cq, -(NA_KC - 1), NA_KC - 1) + (NA_KC - 1)
    onehot = (col_off[..., None] == np.arange(2 * NA_KC - 1)).astype(np.float32)
    tiles = jnp.einsum("hdo,qko->hdqk", rpb.astype(F32), onehot, precision=lax.Precision.HIGHEST)
    tiles = jnp.where(col_ok, tiles, -jnp.inf)
    masked = jnp.full((NA_HEADS, 1, w, w), -jnp.inf, F32)
    tiles = jnp.concatenate([tiles, masked], axis=1)
    tables = []
    for rq0, ws in _na_sub_geometry(rows):
        block_rows = []
        for qi in range(NA_SUB_ROWS):
            rq = rq0 + qi
            rs = min(max(rq - kr // 2, 0), rows - kr)
            picks = []
            for kj in range(NA_WIN_ROWS):
                rk = ws + kj
                picks.append(rk - rq + (NA_KR - 1) if rs <= rk < rs + kr else n_dr)
            block_rows.append(jnp.concatenate([tiles[:, dr] for dr in picks], axis=-1))
        tables.append(jnp.concatenate(block_rows, axis=1))
    t = jnp.stack(tables, axis=1)
    nq, nk = t.shape[2], t.shape[3]
    t = t.reshape(NA_HEADS // 2, 2, 5, nq, nk).transpose(0, 2, 1, 3, 4)
    return t.reshape(NA_HEADS // 2, 5, 2 * nq, nk)


def _na_kernel(rows, q_ref, k_ref, v_ref, tab_ref, o_ref, s_ref, p_ref, vaug_ref):
    n_sub = rows // NA_SUB_ROWS
    nq = NA_SUB_ROWS * GRID_W
    nk = NA_WIN_ROWS * GRID_W
    lane = lax.broadcasted_iota(jnp.int32, (nq, LANES), 1)

    def geometry(g):
        ws = min(max(NA_SUB_ROWS * g - NA_KR // 2, 0), rows - NA_WIN_ROWS)
        kind = g if g < 2 else (g - (n_sub - 5) if g >= n_sub - 2 else 2)
        return g * nq, ws * GRID_W, kind

    def stage_scores(g):
        q0, k0, kind = geometry(g)
        qg = q_ref[q0:q0 + nq, :]
        zero = jnp.zeros_like(qg)
        q2 = jnp.concatenate([jnp.where(lane < NA_HEAD_DIM, qg, zero),
                              jnp.where(lane >= NA_HEAD_DIM, qg, zero)], axis=0)
        s_ref[g % NA_STAGES] = _dot_nt(q2, k_ref[k0:k0 + nk, :]) + tab_ref[0, kind]

    def stage_softmax(g):
        s = s_ref[g % NA_STAGES]
        p_ref[g % NA_STAGES] = jnp.exp(s - jnp.max(s, axis=-1, keepdims=True)).astype(BF16)

    def stage_pv(g):
        q0, k0, _ = geometry(g)
        res = jnp.dot(p_ref[g % NA_STAGES], vaug_ref[k0:k0 + nk, :], preferred_element_type=F32)
        pv = res[:, :LANES] / res[:, LANES:]
        o_ref[q0:q0 + nq, :] = jnp.where(lane < NA_HEAD_DIM, pv[:nq], pv[nq:]).astype(o_ref.dtype)

    vaug_ref[:, :LANES] = v_ref[...]
    vaug_ref[:, LANES:] = jnp.ones(v_ref.shape, BF16)

    for step in range(n_sub + 2):
        if step < n_sub:
            stage_scores(step)
        if 0 <= step - 1 < n_sub:
            stage_softmax(step - 1)
        if 0 <= step - 2 < n_sub:
            stage_pv(step - 2)


def _neighborhood_attention(qkv, batch, rpb):
    t = qkv.shape[1]
    s_len = t // batch
    rows = s_len // GRID_W
    assert rows % NA_SUB_ROWS == 0 and rows >= NA_WIN_ROWS + 2 * NA_SUB_ROWS and NA_KR % 2 == 0
    tables = _na_bias_tables(rpb, rows)
    hp = NA_HEADS // 2
    slab = lambda part: pl.BlockSpec((None, s_len, LANES), lambda pp, bb: (part * hp + pp, bb, 0))
    return pl.pallas_call(
        functools.partial(_na_kernel, rows),
        grid=(hp, batch),
        in_specs=[slab(0), slab(1), slab(2),
                  pl.BlockSpec((1,) + tables.shape[1:], lambda pp, bb: (pp, 0, 0, 0))],
        out_specs=slab(0),
        out_shape=jax.ShapeDtypeStruct((hp, t, LANES), BF16),
        scratch_shapes=[
            pltpu.VMEM((NA_STAGES,) + tables.shape[2:], F32),
            pltpu.VMEM((NA_STAGES,) + tables.shape[2:], BF16),
            pltpu.VMEM((s_len, 2 * LANES), BF16),
        ],
        compiler_params=pltpu.CompilerParams(
            dimension_semantics=("parallel", "arbitrary"), vmem_limit_bytes=VMEM_LIMIT),
        name="neighborhood_attention",
    )(qkv, qkv, qkv, tables)


def _post_kernel(final, o_ref, h_ref, p_ref, wo_ref, gf_ref, wg_ref, wu_ref, wd_ref,
                 gp_ref, wpg_ref, wpp_ref, gfin_ref, out_ref):
    o = jnp.concatenate([o_ref[c] for c in range(N_CB)], axis=-1)
    h = h_ref[...] + jnp.dot(o, wo_ref[...], preferred_element_type=F32)

    n = _rms(h, gf_ref[...], NORM_EPS).astype(BF16)
    acc = h
    for c in range(FFN_HIDDEN // FFN_TF):
        cols = slice(c * FFN_TF, (c + 1) * FFN_TF)
        gate = jnp.dot(n, wg_ref[:, cols], preferred_element_type=F32)
        up = jnp.dot(n, wu_ref[:, cols], preferred_element_type=F32)
        act = (gate * _sigmoid(gate) * up).astype(BF16)
        acc = acc + jnp.dot(act, wd_ref[cols, :], preferred_element_type=F32)
    h = acc

    n = _rms(h, gp_ref[...], NORM_EPS).astype(BF16)
    gate = _sigmoid(jnp.dot(n, wpg_ref[...], preferred_element_type=F32))
    proj = jnp.dot(p_ref[...].astype(BF16), wpp_ref[...], preferred_element_type=F32)
    h = h + gate * proj
    if final:
        h = _rms(h, gfin_ref[...], NORM_EPS)
    out_ref[...] = h


def _post_block(o, h, p, layer, wo, g_ffn, wg, wu, wd, g_ple, wpg, wpp, g_final, final):
    t, d = h.shape
    tok = lambda i: (i, 0)
    const = lambda i: (0, 0)
    resident = lambda a: pl.BlockSpec(a.shape, const, pipeline_mode=pl.Buffered(1))
    return pl.pallas_call(
        functools.partial(_post_kernel, final),
        grid=(t // POST_TM,),
        in_specs=[
            pl.BlockSpec((N_CB, POST_TM, LANES), lambda i: (0, i, 0)),
            pl.BlockSpec((POST_TM, d), tok),
            pl.BlockSpec((None, POST_TM, PLE_DIM), lambda i: (layer, i, 0)),
            resident(wo), resident(g_ffn), resident(wg), resident(wu), resident(wd),
            resident(g_ple), resident(wpg), resident(wpp), resident(g_final),
        ],
        out_specs=pl.BlockSpec((POST_TM, d), tok),
        out_shape=jax.ShapeDtypeStruct((t, d), F32),
        input_output_aliases={1: 0} if layer > 0 else {},
        compiler_params=pltpu.CompilerParams(
            dimension_semantics=("parallel",), vmem_limit_bytes=VMEM_LIMIT),
        name="post_block",
    )(o, h, p, wo, g_ffn, wg, wu, wd, g_ple, wpg, wpp, g_final)


def kernel(x, p, norm_mix, norm_ffn, norm_ple, norm_final, da_w_qkv, da_w_o, da_lambda_q1, da_lambda_k1, da_lambda_q2, da_lambda_k2, da_subln, na_w_qkv, na_b_qkv, na_rpb, na_w_o, ffn_w_gate, ffn_w_up, ffn_w_down, ple_w_proj, ple_w_gate):
    b, s_len, d = x.shape
    t = b * s_len
    row = lambda a: a.reshape(1, -1).astype(F32)
    q_scale = jnp.concatenate([jnp.full((d,), DA_HEAD_DIM ** -0.5, F32), jnp.ones((2 * d,), F32)])
    q_scale = q_scale.reshape(1, -1)
    p = p.reshape(DEPTH, t, PLE_DIM)
    h = x.reshape(t, d)
    for i in range(DEPTH):
        j = i // 2
        if i % 2 == 0:
            qkv = _qkv_proj(h, row(norm_mix[i]), da_w_qkv[j].astype(BF16),
                            jnp.zeros((1, 3 * d), F32), q_scale)
            mix = _diff_attention(qkv, b, da_lambda_q1[j], da_lambda_k1[j],
                                  da_lambda_q2[j], da_lambda_k2[j], da_subln[j], i)
            w_o = da_w_o[j]
        else:
            qkv = _qkv_proj(h, row(norm_mix[i]), na_w_qkv[j].astype(BF16),
                            row(na_b_qkv[j]), q_scale)
            mix = _neighborhood_attention(qkv, b, na_rpb[j])
            w_o = na_w_o[j]
        h = _post_block(mix, h, p, i, w_o.astype(BF16), row(norm_ffn[i]),
                        ffn_w_gate[i].astype(BF16), ffn_w_up[i].astype(BF16),
                        ffn_w_down[i].astype(BF16), row(norm_ple[i]),
                        ple_w_gate[i].astype(BF16), ple_w_proj[i].astype(BF16),
                        row(norm_final), final=(i == DEPTH - 1))
    return h.reshape(b, s_len, d)
```

```python
import functools
import math

import numpy as np
import jax
import jax.numpy as jnp
from jax import lax
from jax.experimental import pallas as pl
from jax.experimental.pallas import tpu as pltpu

F32 = jnp.float32
BF16 = jnp.bfloat16

D_MODEL = 1024
DEPTH = 4
PLE_DIM = 256
FFN_HIDDEN = 2816
DA_HEADS = 8
DA_HEAD_DIM = 64
NA_HEADS = 16
NA_HEAD_DIM = 64
GRID_W = 64
NA_KR = 8
NA_KC = 16
NORM_EPS = 1e-6
SUBLN_EPS = 1e-5

LANES = 128
MXU_WIDTH = 256
VMEM_LIMIT = 56 * 1024 * 1024
N_CB = D_MODEL // LANES

QKV_TM = 1024
QKV_TN = 1024
POST_TM = 512
FFN_TF = MXU_WIDTH
DA_TQ = MXU_WIDTH
DA_RC = 32
DA_TILES = 8
POS_SHIFT = 6
assert (1 << POS_SHIFT) == GRID_W == DA_HEAD_DIM
NA_SUB_ROWS = 2
NA_WIN_ROWS = NA_KR + 2
NA_STAGES = 3


def _rms(x, g, eps):
    return x * lax.rsqrt(jnp.mean(x * x, axis=-1, keepdims=True) + eps) * g


def _sigmoid(x):
    return 1.0 / (1.0 + jnp.exp(-x))


def _dot_nt(a, b):
    return lax.dot_general(a, b, (((1,), (1,)), ((), ())), preferred_element_type=F32)


def _qkv_kernel(x_ref, g_ref, w_ref, b_ref, cs_ref, o_ref):
    half = QKV_TM // 2
    n_chunks = w_ref.shape[1] // QKV_TN
    cb = QKV_TN // LANES
    norm = lambda rows: _rms(x_ref[rows, :], g_ref[...], NORM_EPS).astype(BF16)
    xn = [norm(slice(0, half)), None]
    for hf in range(2):
        rows = slice(hf * half, (hf + 1) * half)
        for j in range(n_chunks):
            cols = slice(j * QKV_TN, (j + 1) * QKV_TN)
            acc = jnp.dot(xn[hf], w_ref[:, cols], preferred_element_type=F32)
            if hf == 0 and j == 0:
                xn[1] = norm(slice(half, 2 * half))
            res = ((acc + b_ref[:, cols]) * cs_ref[:, cols]).astype(o_ref.dtype)
            for c in range(cb):
                o_ref[j * cb + c, rows, :] = res[:, c * LANES:(c + 1) * LANES]


def _qkv_proj(h, g, w, b, cs):
    t, d = h.shape
    n = w.shape[1]
    const = lambda i: (0, 0)
    resident = lambda a: pl.BlockSpec(a.shape, const, pipeline_mode=pl.Buffered(1))
    return pl.pallas_call(
        _qkv_kernel,
        grid=(t // QKV_TM,),
        in_specs=[
            pl.BlockSpec((QKV_TM, d), lambda i: (i, 0)),
            resident(g), resident(w), resident(b), resident(cs),
        ],
        out_specs=pl.BlockSpec((n // LANES, QKV_TM, LANES), lambda i: (0, i, 0)),
        out_shape=jax.ShapeDtypeStruct((n // LANES, t, LANES), BF16),
        compiler_params=pltpu.CompilerParams(
            dimension_semantics=("parallel",), vmem_limit_bytes=VMEM_LIMIT),
        name="qkv_proj",
    )(h, g, w, b, cs)


def _da_key_features(s_len):
    pos = np.arange(s_len)
    f = np.zeros((s_len, LANES), np.float32)
    for base in (0, DA_HEAD_DIM):
        f[:, base + 0] = pos // GRID_W
        f[:, base + 1] = pos % GRID_W
        f[:, base + 2] = 1.0
        f[:, base + 3] = 1.0
    return jnp.asarray(f, BF16)


def _da_diag_correction():
    r = np.arange(DA_TQ)
    return jnp.asarray(-2.0 * np.maximum(r[None, :] - r[:, None], 0), F32)


def _da_kernel(lam_init, nb, slopes_ref, qc_ref, qn_ref, k_ref, v_ref, kfeat_ref, corr_ref,
               lq1_ref, lk1_ref, lq2_ref, lk2_ref, sub_ref, o_ref,
               kaug_ref, vaug_ref, qv_ref, sa_ref, sb_ref, pa_ref, pb_ref):
    h = pl.program_id(1)
    t = pl.program_id(2)
    tq = DA_TQ
    slope = slopes_ref[h]
    corr = slope * corr_ref[...]

    def scores(q, i_tile, slot, s_ref):
        lane = lax.broadcasted_iota(jnp.int32, q.shape, 1)
        fidx = jnp.bitwise_and(lane, DA_HEAD_DIM - 1)
        pos = i_tile * tq + lax.broadcasted_iota(jnp.int32, q.shape, 0)
        pos_hi = jnp.right_shift(pos, POS_SHIFT).astype(F32)
        pos_lo = jnp.bitwise_and(pos, GRID_W - 1).astype(F32)
        feat = jnp.where(fidx == 0, GRID_W * slope,
                         jnp.where(fidx == 1, slope,
                                   jnp.where(fidx == 2, -GRID_W * slope * pos_hi,
                                             jnp.where(fidx == 3, -slope * pos_lo, 0.0))))
        feat_l = feat.astype(BF16)
        feat_r = (-feat).astype(BF16)
        for m in range(2):
            is_data = (lane < DA_HEAD_DIM) if m == 0 else (lane >= DA_HEAD_DIM)
            qv_ref[slot, m, 0] = jnp.where(is_data, q, feat_l)
            qv_ref[slot, m, 1] = jnp.where(is_data, q, feat_r)

        def block(m, j):
            side = jnp.where(j > i_tile, 1, 0)
            s_ref[m, j] = _dot_nt(qv_ref[slot, m, side], kaug_ref[m, j * tq:(j + 1) * tq, :])
            if j == nb - 1:
                s_ref[m, i_tile] = s_ref[m, i_tile] + corr

        return [functools.partial(block, m, j) for m in range(2) for j in range(nb)]

    def softmax_chunk(s_ref, p_ref, m, rc):
        rows = slice(rc * DA_RC, (rc + 1) * DA_RC)
        m_el = s_ref[m, 0, rows, :]
        for j in range(1, nb):
            m_el = jnp.maximum(m_el, s_ref[m, j, rows, :])
        mb = jnp.broadcast_to(jnp.max(m_el, axis=-1, keepdims=True), m_el.shape)
        for j in range(nb):
            p_ref[m, rows, j * tq:(j + 1) * tq] = jnp.exp(s_ref[m, j, rows, :] - mb).astype(BF16)

    def softmax_pv(s_ref, p_ref, mxu_tasks):
        n_chunks = tq // DA_RC
        per_chunk = len(mxu_tasks) // (2 * n_chunks)
        outs = []
        for m in range(2):
            for rc in range(n_chunks):
                softmax_chunk(s_ref, p_ref, m, rc)
                for _ in range(per_chunk):
                    mxu_tasks.pop(0)()
            res = jnp.dot(p_ref[m], vaug_ref[...], preferred_element_type=F32)
            outs.append(res[:, :LANES] / res[:, LANES:])
        while mxu_tasks:
            mxu_tasks.pop(0)()
        lam = (jnp.exp(jnp.sum(lq1_ref[...] * lk1_ref[...], axis=-1, keepdims=True))
               - jnp.exp(jnp.sum(lq2_ref[...] * lk2_ref[...], axis=-1, keepdims=True))
               + lam_init)
        o = outs[0] - lam * outs[1]
        o = _rms(o, sub_ref[...], SUBLN_EPS) * (1.0 - lam_init)
        return o.astype(o_ref.dtype)

    @pl.when(t == 0)
    def _():
        k = k_ref[...]
        kf = kfeat_ref[...]
        lane_k = lax.broadcasted_iota(jnp.int32, k.shape, 1)
        kaug_ref[0] = jnp.where(lane_k < DA_HEAD_DIM, k, kf)
        kaug_ref[1] = jnp.where(lane_k >= DA_HEAD_DIM, k, kf)
        vaug_ref[:, :LANES] = v_ref[...]
        vaug_ref[:, LANES:] = jnp.ones(v_ref.shape, BF16)
        for task in scores(qc_ref[:tq, :], DA_TILES * t, 0, sa_ref):
            task()

    s_bufs, p_bufs = (sa_ref, sb_ref), (pa_ref, pb_ref)
    for n in range(DA_TILES):
        if n + 1 < DA_TILES:
            q_next, i_next = qc_ref[(n + 1) * tq:(n + 2) * tq, :], DA_TILES * t + n + 1
        else:
            q_next, i_next = qn_ref[...], jnp.minimum(DA_TILES * (t + 1), nb - 1)
        tasks = scores(q_next, i_next, n + 1, s_bufs[(n + 1) % 2])
        o_ref[n * tq:(n + 1) * tq, :] = softmax_pv(s_bufs[n % 2], p_bufs[n % 2], tasks)


def _diff_attention(qkv, batch, lq1, lk1, lq2, lk2, subln, layer_idx):
    t = qkv.shape[1]
    s_len = t // batch
    nb = s_len // DA_TQ
    lam_init = 0.8 - 0.6 * math.exp(-0.3 * layer_idx)
    slopes = [2.0 ** (-8.0 * (hh + 1) / DA_HEADS) for hh in range(DA_HEADS)]
    for sl in slopes:
        assert float(np.asarray(sl * GRID_W * (GRID_W - 1), dtype=BF16)) == sl * GRID_W * (GRID_W - 1)
    assert s_len <= GRID_W * GRID_W and nb % DA_TILES == 0 and DA_TILES % 2 == 0
    steps = nb // DA_TILES
    step_rows = DA_TILES * DA_TQ
    vec = lambda a: a.reshape(1, -1).astype(F32)
    small = pl.BlockSpec((1, DA_HEAD_DIM), lambda bb, hh, ii: (0, 0))
    return pl.pallas_call(
        functools.partial(_da_kernel, lam_init, nb),
        grid=(batch, DA_HEADS, steps),
        in_specs=[
            pl.BlockSpec(memory_space=pltpu.SMEM),
            pl.BlockSpec((None, step_rows, LANES), lambda bb, hh, ii: (hh, bb * steps + ii, 0)),
            pl.BlockSpec((None, DA_TQ, LANES),
                         lambda bb, hh, ii: (hh, bb * nb + jnp.minimum(DA_TILES * (ii + 1), nb - 1), 0)),
            pl.BlockSpec((None, s_len, LANES), lambda bb, hh, ii: (DA_HEADS + hh, bb, 0)),
            pl.BlockSpec((None, s_len, LANES), lambda bb, hh, ii: (2 * DA_HEADS + hh, bb, 0)),
            pl.BlockSpec((s_len, LANES), lambda bb, hh, ii: (0, 0)),
            pl.BlockSpec((DA_TQ, DA_TQ), lambda bb, hh, ii: (0, 0)),
            small, small, small, small,
            pl.BlockSpec((1, LANES), lambda bb, hh, ii: (0, 0)),
        ],
        out_specs=pl.BlockSpec((None, step_rows, LANES), lambda bb, hh, ii: (hh, bb * steps + ii, 0)),
        out_shape=jax.ShapeDtypeStruct((DA_HEADS, t, LANES), BF16),
        scratch_shapes=[
            pltpu.VMEM((2, s_len, LANES), BF16),
            pltpu.VMEM((s_len, 2 * LANES), BF16),
            pltpu.VMEM((DA_TILES + 1, 2, 2, DA_TQ, LANES), BF16),
            pltpu.VMEM((2, nb, DA_TQ, DA_TQ), F32),
            pltpu.VMEM((2, nb, DA_TQ, DA_TQ), F32),
            pltpu.VMEM((2, DA_TQ, s_len), BF16),
            pltpu.VMEM((2, DA_TQ, s_len), BF16),
        ],
        compiler_params=pltpu.CompilerParams(
            dimension_semantics=("parallel", "parallel", "arbitrary"), vmem_limit_bytes=VMEM_LIMIT),
        name="diff_attention",
    )(jnp.asarray(slopes, F32), qkv, qkv, qkv, qkv, _da_key_features(s_len), _da_diag_correction(),
      vec(lq1), vec(lk1), vec(lq2), vec(lk2), vec(subln))


def _na_sub_geometry(rows):
    last = rows - NA_WIN_ROWS
    return ((0, 0), (NA_SUB_ROWS, 0), (2 * NA_SUB_ROWS, 0),
            (rows - 2 * NA_SUB_ROWS, last), (rows - NA_SUB_ROWS, last))


def _na_bias_tables(rpb, rows):
    w = GRID_W
    kr = min(NA_KR, rows)
    n_dr = 2 * NA_KR - 1
    cq, ck = np.arange(w)[:, None], np.arange(w)[None, :]
    cstart = np.clip(cq - NA_KC // 2, 0, w - NA_KC)
    col_ok = (ck >= cstart) & (ck < cstart + NA_KC)
    col_off = np.clip(ck - cq, -(NA_KC - 1), NA_KC - 1) + (NA_KC - 1)
    onehot = (col_off[..., None] == np.arange(2 * NA_KC - 1)).astype(np.float32)
    tiles = jnp.einsum("hdo,qko->hdqk", rpb.astype(F32), onehot, precision=lax.Precision.HIGHEST)
    tiles = jnp.where(col_ok, tiles, -jnp.inf)
    masked = jnp.full((NA_HEADS, 1, w, w), -jnp.inf, F32)
    tiles = jnp.concatenate([tiles, masked], axis=1)
    tables = []
    for rq0, ws in _na_sub_geometry(rows):
        block_rows = []
        for qi in range(NA_SUB_ROWS):
            rq = rq0 + qi
            rs = min(max(rq - kr // 2, 0), rows - kr)
            picks = []
            for kj in range(NA_WIN_ROWS):
                rk = ws + kj
                picks.append(rk - rq + (NA_KR - 1) if rs <= rk < rs + kr else n_dr)
            block_rows.append(jnp.concatenate([tiles[:, dr] for dr in picks], axis=-1))
        tables.append(jnp.concatenate(block_rows, axis=1))
    t = jnp.stack(tables, axis=1)
    nq, nk = t.shape[2], t.shape[3]
    t = t.reshape(NA_HEADS // 2, 2, 5, nq, nk).transpose(0, 2, 1, 3, 4)
    return t.reshape(NA_HEADS // 2, 5, 2 * nq, nk)


def _na_kernel(rows, q_ref, k_ref, v_ref, tab_ref, o_ref, s_ref, p_ref, vaug_ref):
    n_sub = rows // NA_SUB_ROWS
    nq = NA_SUB_ROWS * GRID_W
    nk = NA_WIN_ROWS * GRID_W
    lane = lax.broadcasted_iota(jnp.int32, (nq, LANES), 1)

    def geometry(g):
        ws = min(max(NA_SUB_ROWS * g - NA_KR // 2, 0), rows - NA_WIN_ROWS)
        kind = g if g < 2 else (g - (n_sub - 5) if g >= n_sub - 2 else 2)
        return g * nq, ws * GRID_W, kind

    def stage_scores(g):
        q0, k0, kind = geometry(g)
        qg = q_ref[q0:q0 + nq, :]
        zero = jnp.zeros_like(qg)
        q2 = jnp.concatenate([jnp.where(lane < NA_HEAD_DIM, qg, zero),
                              jnp.where(lane >= NA_HEAD_DIM, qg, zero)], axis=0)
        s_ref[g % NA_STAGES] = _dot_nt(q2, k_ref[k0:k0 + nk, :]) + tab_ref[0, kind]

    def stage_softmax(g):
        s = s_ref[g % NA_STAGES]
        p_ref[g % NA_STAGES] = jnp.exp(s - jnp.max(s, axis=-1, keepdims=True)).astype(BF16)

    def stage_pv(g):
        q0, k0, _ = geometry(g)
        res = jnp.dot(p_ref[g % NA_STAGES], vaug_ref[k0:k0 + nk, :], preferred_element_type=F32)
        pv = res[:, :LANES] / res[:, LANES:]
        o_ref[q0:q0 + nq, :] = jnp.where(lane < NA_HEAD_DIM, pv[:nq], pv[nq:]).astype(o_ref.dtype)

    vaug_ref[:, :LANES] = v_ref[...]
    vaug_ref[:, LANES:] = jnp.ones(v_ref.shape, BF16)

    for step in range(n_sub + 2):
        if step < n_sub:
            stage_scores(step)
        if 0 <= step - 1 < n_sub:
            stage_softmax(step - 1)
        if 0 <= step - 2 < n_sub:
            stage_pv(step - 2)


def _neighborhood_attention(qkv, batch, rpb):
    t = qkv.shape[1]
    s_len = t // batch
    rows = s_len // GRID_W
    assert rows % NA_SUB_ROWS == 0 and rows >= NA_WIN_ROWS + 2 * NA_SUB_ROWS and NA_KR % 2 == 0
    tables = _na_bias_tables(rpb, rows)
    hp = NA_HEADS // 2
    slab = lambda part: pl.BlockSpec((None, s_len, LANES), lambda pp, bb: (part * hp + pp, bb, 0))
    return pl.pallas_call(
        functools.partial(_na_kernel, rows),
        grid=(hp, batch),
        in_specs=[slab(0), slab(1), slab(2),
                  pl.BlockSpec((1,) + tables.shape[1:], lambda pp, bb: (pp, 0, 0, 0))],
        out_specs=slab(0),
        out_shape=jax.ShapeDtypeStruct((hp, t, LANES), BF16),
        scratch_shapes=[
            pltpu.VMEM((NA_STAGES,) + tables.shape[2:], F32),
            pltpu.VMEM((NA_STAGES,) + tables.shape[2:], BF16),
            pltpu.VMEM((s_len, 2 * LANES), BF16),
        ],
        compiler_params=pltpu.CompilerParams(
            dimension_semantics=("parallel", "arbitrary"), vmem_limit_bytes=VMEM_LIMIT),
        name="neighborhood_attention",
    )(qkv, qkv, qkv, tables)


def _post_kernel(final, o_ref, h_ref, p_ref, wo_ref, gf_ref, wg_ref, wu_ref, wd_ref,
                 gp_ref, wpg_ref, wpp_ref, gfin_ref, out_ref):
    o = jnp.concatenate([o_ref[c] for c in range(N_CB)], axis=-1)
    h = h_ref[...] + jnp.dot(o, wo_ref[...], preferred_element_type=F32)

    n = _rms(h, gf_ref[...], NORM_EPS).astype(BF16)
    acc = h
    for c in range(FFN_HIDDEN // FFN_TF):
        cols = slice(c * FFN_TF, (c + 1) * FFN_TF)
        gate = jnp.dot(n, wg_ref[:, cols], preferred_element_type=F32)
        up = jnp.dot(n, wu_ref[:, cols], preferred_element_type=F32)
        act = (gate * _sigmoid(gate) * up).astype(BF16)
        acc = acc + jnp.dot(act, wd_ref[cols, :], preferred_element_type=F32)
    h = acc

    n = _rms(h, gp_ref[...], NORM_EPS).astype(BF16)
    gate = _sigmoid(jnp.dot(n, wpg_ref[...], preferred_element_type=F32))
    proj = jnp.dot(p_ref[...].astype(BF16), wpp_ref[...], preferred_element_type=F32)
    h = h + gate * proj
    if final:
        h = _rms(h, gfin_ref[...], NORM_EPS)
    out_ref[...] = h


def _post_block(o, h, p, layer, wo, g_ffn, wg, wu, wd, g_ple, wpg, wpp, g_final, final):
    t, d = h.shape
    tok = lambda i: (i, 0)
    const = lambda i: (0, 0)
    resident = lambda a: pl.BlockSpec(a.shape, const, pipeline_mode=pl.Buffered(1))
    return pl.pallas_call(
        functools.partial(_post_kernel, final),
        grid=(t // POST_TM,),
        in_specs=[
            pl.BlockSpec((N_CB, POST_TM, LANES), lambda i: (0, i, 0)),
            pl.BlockSpec((POST_TM, d), tok),
            pl.BlockSpec((None, POST_TM, PLE_DIM), lambda i: (layer, i, 0)),
            resident(wo), resident(g_ffn), resident(wg), resident(wu), resident(wd),
            resident(g_ple), resident(wpg), resident(wpp), resident(g_final),
        ],
        out_specs=pl.BlockSpec((POST_TM, d), tok),
        out_shape=jax.ShapeDtypeStruct((t, d), F32),
        input_output_aliases={1: 0} if layer > 0 else {},
        compiler_params=pltpu.CompilerParams(
            dimension_semantics=("parallel",), vmem_limit_bytes=VMEM_LIMIT),
        name="post_block",
    )(o, h, p, wo, g_ffn, wg, wu, wd, g_ple, wpg, wpp, g_final)


def kernel(x, p, norm_mix, norm_ffn, norm_ple, norm_final, da_w_qkv, da_w_o, da_lambda_q1, da_lambda_k1, da_lambda_q2, da_lambda_k2, da_subln, na_w_qkv, na_b_qkv, na_rpb, na_w_o, ffn_w_gate, ffn_w_up, ffn_w_down, ple_w_proj, ple_w_gate):
    b, s_len, d = x.shape
    t = b * s_len
    row = lambda a: a.reshape(1, -1).astype(F32)
    q_scale = jnp.concatenate([jnp.full((d,), DA_HEAD_DIM ** -0.5, F32), jnp.ones((2 * d,), F32)])
    q_scale = q_scale.reshape(1, -1)
    p = p.reshape(DEPTH, t, PLE_DIM)
    h = x.reshape(t, d)
    for i in range(DEPTH):
        j = i // 2
        if i % 2 == 0:
            qkv = _qkv_proj(h, row(norm_mix[i]), da_w_qkv[j].astype(BF16),
                            jnp.zeros((1, 3 * d), F32), q_scale)
            mix = _diff_attention(qkv, b, da_lambda_q1[j], da_lambda_k1[j],
                                  da_lambda_q2[j], da_lambda_k2[j], da_subln[j], i)
            w_o = da_w_o[j]
        else:
            qkv = _qkv_proj(h, row(norm_mix[i]), na_w_qkv[j].astype(BF16),
                            row(na_b_qkv[j]), q_scale)
            mix = _neighborhood_attention(qkv, b, na_rpb[j])
            w_o = na_w_o[j]
        h = _post_block(mix, h, p, i, w_o.astype(BF16), row(norm_ffn[i]),
                        ffn_w_gate[i].astype(BF16), ffn_w_up[i].astype(BF16),
                        ffn_w_down[i].astype(BF16), row(norm_ple[i]),
                        ple_w_gate[i].astype(BF16), ple_w_proj[i].astype(BF16),
                        row(norm_final), final=(i == DEPTH - 1))
    return h.reshape(b, s_len, d)
```
